```python
import jax, jax.numpy as jnp
from jax import lax
import numpy as np

D_MODEL = 2048
BATCH = 8
SEQ = 4096
DEPTH = 2

EXPAND = 2
D_INNER = EXPAND * D_MODEL
N_MIXERS = 2
N_GLA = (DEPTH + 1) // 2
N_RET = DEPTH // 2
EPS = 1e-6
GLA_HEADS = 4
GLA_DK = D_MODEL // 2
GLA_HEAD_K = GLA_DK // GLA_HEADS
GLA_HEAD_V = D_INNER // GLA_HEADS
GLA_GATE_RANK = 16
GLA_GATE_TEMP = 16.0
GLA_CHUNK = 64
RET_HEADS = 8
RET_DK = D_MODEL
RET_HEAD_K = RET_DK // RET_HEADS
RET_HEAD_V = D_INNER // RET_HEADS
RET_CHUNK = 64
ROPE_BASE = 10000.0

kernel_name = "hybrid_gla_retnet_trunk"


def rms_norm(x, g):
    xf = x.astype(jnp.float32)
    xn = xf * lax.rsqrt(jnp.mean(xf * xf, axis=-1, keepdims=True) + EPS)
    return xn.astype(x.dtype) * g


def to_chunks(t, c):
    b, s, h, d = t.shape
    return t.reshape(b, s // c, c, h, d).transpose(1, 0, 3, 2, 4)


def from_chunks(t):
    nc, b, h, c, d = t.shape
    return t.transpose(1, 0, 3, 2, 4).reshape(b, nc * c, h, d)


def gla_mixer(h, w_in, w_a1, w_a2, b_a, head_g, w_out):
    B, S, _ = h.shape
    proj = h @ w_in
    q, k, v, gate = jnp.split(proj, [GLA_DK, 2 * GLA_DK, 2 * GLA_DK + D_INNER], axis=-1)
    log_a = jax.nn.log_sigmoid(((h @ w_a1) @ w_a2 + b_a).astype(jnp.float32)) / GLA_GATE_TEMP
    hd = lambda t, d: t.reshape(B, S, GLA_HEADS, d).astype(jnp.float32)
    qc = to_chunks(hd(q, GLA_HEAD_K) * (GLA_HEAD_K ** -0.5), GLA_CHUNK)
    kc = to_chunks(hd(k, GLA_HEAD_K), GLA_CHUNK)
    vc = to_chunks(hd(v, GLA_HEAD_V), GLA_CHUNK)
    ac = to_chunks(hd(log_a, GLA_HEAD_K), GLA_CHUNK)
    causal = jnp.tril(jnp.ones((GLA_CHUNK, GLA_CHUNK), dtype=bool))[:, :, None]

    def step(state, inp):
        qb, kb, vb, ab = inp
        cum = jnp.cumsum(ab, axis=-2)
        last = cum[..., -1:, :]
        inter = jnp.einsum('bhtk,bhkv->bhtv', qb * jnp.exp(cum), state)
        diff = cum[:, :, :, None, :] - cum[:, :, None, :, :]
        decay = jnp.where(causal, jnp.exp(jnp.where(causal, diff, 0.0)), 0.0)
        scores = jnp.einsum('bhtk,bhsk,bhtsk->bhts', qb, kb, decay)
        intra = jnp.einsum('bhts,bhsv->bhtv', scores, vb)
        new_state = jnp.exp(last)[:, :, 0, :, None] * state + jnp.einsum(
            'bhsk,bhsv->bhkv', kb * jnp.exp(last - cum), vb)
        return new_state, inter + intra

    s0 = jnp.zeros((B, GLA_HEADS, GLA_HEAD_K, GLA_HEAD_V), jnp.float32)
    _, o = lax.scan(step, s0, (qc, kc, vc, ac))
    o = from_chunks(o)
    o = o * lax.rsqrt(jnp.mean(o * o, axis=-1, keepdims=True) + EPS)
    o = o.reshape(B, S, D_INNER).astype(h.dtype) * head_g
    return (o * jax.nn.silu(gate)) @ w_out


def apply_rotary(t, cos, sin):
    t1, t2 = jnp.split(t, 2, axis=-1)
    return jnp.concatenate([t1 * cos - t2 * sin, t2 * cos + t1 * sin], axis=-1)


def retention_mixer(h, positions, w_in, gn_g, gn_b, w_out):
    B, S, _ = h.shape
    proj = h @ w_in
    q, k, v, gate = jnp.split(proj, [RET_DK, 2 * RET_DK, 2 * RET_DK + D_INNER], axis=-1)
    q = q.reshape(B, S, RET_HEADS, RET_HEAD_K).astype(jnp.float32)
    k = k.reshape(B, S, RET_HEADS, RET_HEAD_K).astype(jnp.float32) * (RET_HEAD_K ** -0.5)
    v = v.reshape(B, S, RET_HEADS, RET_HEAD_V).astype(jnp.float32)
    inv_freq = ROPE_BASE ** (-jnp.arange(RET_HEAD_K // 2, dtype=jnp.float32) / (RET_HEAD_K // 2))
    ang = positions.astype(jnp.float32)[..., None] * inv_freq
    cos, sin = jnp.cos(ang)[:, :, None, :], jnp.sin(ang)[:, :, None, :]
    q, k = apply_rotary(q, cos, sin), apply_rotary(k, cos, sin)

    log_gamma = jnp.log1p(-jnp.exp2(-5.0 - jnp.arange(RET_HEADS, dtype=jnp.float32)))
    idx = jnp.arange(RET_CHUNK, dtype=jnp.float32)
    dpos = idx[:, None] - idx[None, :]
    dmat = jnp.where(dpos >= 0, jnp.exp(log_gamma[:, None, None] * jnp.where(dpos >= 0, dpos, 0.0)), 0.0)
    xi = jnp.exp(log_gamma[:, None] * (idx + 1.0))[..., None]
    zeta = jnp.exp(log_gamma[:, None] * (RET_CHUNK - 1.0 - idx))[..., None]
    g_chunk = jnp.exp(log_gamma * RET_CHUNK)[:, None, None]

    def step(state, inp):
        qb, kb, vb = inp
        scores = jnp.einsum('bhtk,bhsk->bhts', qb, kb) * dmat
        intra = jnp.einsum('bhts,bhsv->bhtv', scores, vb)
        inter = jnp.einsum('bhtk,bhkv->bhtv', qb, state) * xi
        new_state = g_chunk * state + jnp.einsum('bhsk,bhsv->bhkv', kb * zeta, vb)
        return new_state, intra + inter

    s0 = jnp.zeros((B, RET_HEADS, RET_HEAD_K, RET_HEAD_V), jnp.float32)
    _, o = lax.scan(step, s0, (to_chunks(q, RET_CHUNK), to_chunks(k, RET_CHUNK), to_chunks(v, RET_CHUNK)))
    o = from_chunks(o)
    mu = jnp.mean(o, axis=-1, keepdims=True)
    var = jnp.mean(jnp.square(o - mu), axis=-1, keepdims=True)
    o = ((o - mu) * lax.rsqrt(var + EPS)).reshape(B, S, D_INNER).astype(h.dtype)
    o = o * gn_g + gn_b
    return (o * jax.nn.silu(gate)) @ w_out


def setup_inputs(seed: int = 0) -> dict:
    key = jax.random.key(seed)
    ks = jax.random.split(key, 16)
    nrm = lambda k, shape, fan_in: jax.random.normal(k, shape, jnp.float32) * fan_in ** -0.5
    x = jax.random.normal(ks[0], (BATCH, SEQ, D_MODEL), jnp.float32)
    offsets = jax.random.randint(ks[1], (BATCH, 1), 0, 1024, dtype=jnp.int32)
    positions = offsets + jnp.arange(SEQ, dtype=jnp.int32)[None, :]
    gla_in = 2 * GLA_DK + 2 * D_INNER
    ret_in = 2 * RET_DK + 2 * D_INNER
    return {
        "x": x,
        "positions": positions,
        "gla_norm": 1.0 + 0.02 * jax.random.normal(ks[2], (N_GLA, D_MODEL), jnp.float32),
        "gla_w_in": nrm(ks[3], (N_GLA, D_MODEL, gla_in), D_MODEL),
        "gla_w_a1": nrm(ks[4], (N_GLA, D_MODEL, GLA_GATE_RANK), D_MODEL),
        "gla_w_a2": nrm(ks[5], (N_GLA, GLA_GATE_RANK, GLA_DK), GLA_GATE_RANK),
        "gla_b_a": 0.1 * jax.random.normal(ks[6], (N_GLA, GLA_DK), jnp.float32),
        "gla_head_g": 1.0 + 0.02 * jax.random.normal(ks[7], (N_GLA, D_INNER), jnp.float32),
        "gla_w_out": nrm(ks[8], (N_GLA, D_INNER, D_MODEL), D_INNER),
        "ret_norm": 1.0 + 0.02 * jax.random.normal(ks[9], (N_RET, D_MODEL), jnp.float32),
        "ret_w_in": nrm(ks[10], (N_RET, D_MODEL, ret_in), D_MODEL),
        "ret_gn_g": 1.0 + 0.02 * jax.random.normal(ks[11], (N_RET, D_INNER), jnp.float32),
        "ret_gn_b": 0.02 * jax.random.normal(ks[12], (N_RET, D_INNER), jnp.float32),
        "ret_w_out": nrm(ks[13], (N_RET, D_INNER, D_MODEL), D_INNER),
        "final_norm": 1.0 + 0.02 * jax.random.normal(ks[14], (D_MODEL,), jnp.float32),
    }


def reference(x, positions, gla_norm, gla_w_in, gla_w_a1, gla_w_a2, gla_b_a, gla_head_g, gla_w_out,
              ret_norm, ret_w_in, ret_gn_g, ret_gn_b, ret_w_out, final_norm):
    h = x
    for i in range(DEPTH):
        j = i // N_MIXERS
        if i % N_MIXERS == 0:
            h = h + gla_mixer(rms_norm(h, gla_norm[j]), gla_w_in[j], gla_w_a1[j], gla_w_a2[j],
                              gla_b_a[j], gla_head_g[j], gla_w_out[j])
        else:
            h = h + retention_mixer(rms_norm(h, ret_norm[j]), positions, ret_w_in[j],
                                    ret_gn_g[j], ret_gn_b[j], ret_w_out[j])
    return rms_norm(h, final_norm)
```

```python
import functools

import jax
import jax.numpy as jnp
from jax import lax
from jax.experimental import pallas as pl
from jax.experimental.pallas import tpu as pltpu

F32 = jnp.float32
BF16 = jnp.bfloat16

EPS = 1e-6
GLA_HEADS = 4
GLA_GATE_RANK_PAD = 128
GLA_GATE_TEMP = 16.0
RET_HEADS = 8
ROPE_BASE = 10000.0

CHUNK = 256
FAST_DECAY_LIMIT = 60.0
VMEM_LIMIT_BYTES = 56 * 1024 * 1024

_NT = (((1,), (1,)), ((), ()))
_TN = (((0,), (0,)), ((), ()))


def _dot(a, b):
    return jnp.dot(a, b, preferred_element_type=F32)


def _rms_rows(x, g):
    ms = jnp.mean(x * x, axis=-1, keepdims=True)
    return x * lax.rsqrt(ms + EPS) * g


def _silu(g):
    return g / (1.0 + jnp.exp(-g))


def _split3(a):
    hi = a.astype(BF16)
    r1 = a - hi.astype(F32)
    mid = r1.astype(BF16)
    lo = (r1 - mid.astype(F32)).astype(BF16)
    return hi, mid, lo


def _gla_in_kernel(x_ref, g_ref, w_ref, wa1_ref, wa2_ref, ba_ref, proj_ref, la_ref, hn_ref):
    @pl.when(pl.program_id(1) == 0)
    def _():
        hn = _rms_rows(x_ref[...], g_ref[...]).astype(BF16)
        hn_ref[...] = hn
        z = _dot(hn, wa1_ref[...])
        xa = _dot(z.astype(BF16), wa2_ref[...]) + ba_ref[...]
        la_ref[...] = (jnp.minimum(xa, 0.0) - jnp.log1p(jnp.exp(-jnp.abs(xa)))) * (1.0 / GLA_GATE_TEMP)

    proj_ref[...] = _dot(hn_ref[...], w_ref[...]).astype(BF16)


def _gla_in_proj(x2, g, w, wa1, wa2, ba, *, tm, tn):
    m, d = x2.shape
    n = w.shape[1]
    dk = wa2.shape[1]
    return pl.pallas_call(
        _gla_in_kernel,
        grid=(m // tm, n // tn),
        in_specs=[
            pl.BlockSpec((tm, d), lambda i, j: (i, 0)),
            pl.BlockSpec((1, d), lambda i, j: (0, 0)),
            pl.BlockSpec((d, tn), lambda i, j: (0, j)),
            pl.BlockSpec((d, GLA_GATE_RANK_PAD), lambda i, j: (0, 0)),
            pl.BlockSpec((GLA_GATE_RANK_PAD, dk), lambda i, j: (0, 0)),
            pl.BlockSpec((1, dk), lambda i, j: (0, 0)),
        ],
        out_specs=[
            pl.BlockSpec((tm, tn), lambda i, j: (i, j)),
            pl.BlockSpec((tm, dk), lambda i, j: (i, 0)),
        ],
        out_shape=[
            jax.ShapeDtypeStruct((m, n), BF16),
            jax.ShapeDtypeStruct((m, dk), F32),
        ],
        scratch_shapes=[pltpu.VMEM((tm, d), BF16)],
        compiler_params=pltpu.CompilerParams(
            dimension_semantics=("arbitrary", "arbitrary"), vmem_limit_bytes=VMEM_LIMIT_BYTES),
        name="gla_in_proj",
    )(x2, g, w, wa1, wa2, ba)


def _gla_rec_kernel(q_ref, k_ref, v_ref, gate_ref, la_ref, hg_ref, o_ref,
                    state_ref, p_ref, cum_ref, kf_ref, sc_ref, *, q_scale):
    c, dk = q_ref.shape
    dv = v_ref.shape[1]

    @pl.when(pl.program_id(2) == 0)
    def _():
        state_ref[...] = jnp.zeros_like(state_ref)

    row = lax.broadcasted_iota(jnp.int32, (c, c), 0)
    col = lax.broadcasted_iota(jnp.int32, (c, c), 1)
    causal = row >= col
    ltri = jnp.where(causal, 1.0, 0.0).astype(BF16)
    ones = jnp.ones((c, 128), BF16)

    a_hi, a_mid, a_lo = _split3(la_ref[...])
    cum = _dot(ltri, a_hi) + _dot(ltri, a_mid) + _dot(ltri, a_lo)
    tot_col = (lax.dot_general(a_hi, ones, _TN, preferred_element_type=F32)
               + lax.dot_general(a_mid, ones, _TN, preferred_element_type=F32)
               + lax.dot_general(a_lo, ones, _TN, preferred_element_type=F32))
    total = cum[c - 1:c, :]

    qf = q_ref[...].astype(F32) * q_scale
    kf = k_ref[...].astype(F32)
    v = v_ref[...]
    qs = (qf * jnp.exp(cum)).astype(BF16)
    ks = (kf * jnp.exp(total - cum)).astype(BF16)

    fast = jnp.min(total) >= -FAST_DECAY_LIMIT

    @pl.when(fast)
    def _():
        kn = (kf * jnp.exp(-cum)).astype(BF16)
        s = lax.dot_general(qs, kn, _NT, preferred_element_type=F32)
        p_ref[...] = jnp.where(causal, s, 0.0).astype(BF16)

    @pl.when(jnp.logical_not(fast))
    def _():
        cum_ref[...] = cum
        kf_ref[...] = kf
        sc_ref[...] = jnp.zeros_like(sc_ref)
        lane = lax.broadcasted_iota(jnp.int32, (1, c), 1)

        def column(s, carry):
            d = jnp.exp(jnp.minimum(cum - cum_ref[pl.ds(s, 1), :], 0.0))
            colv = jnp.sum(qf * kf_ref[pl.ds(s, 1), :] * d, axis=1, keepdims=True)
            sc_ref[...] += colv * jnp.where(lane == s, 1.0, 0.0)
            return carry

        lax.fori_loop(0, c, column, 0)
        p_ref[...] = jnp.where(causal, sc_ref[...], 0.0).astype(BF16)

    st = state_ref[...]
    o = _dot(qs, st.astype(BF16)) + _dot(p_ref[...], v)
    decay = jnp.exp(tot_col)
    decay = jnp.concatenate([decay] * (dv // 128), axis=1)
    state_ref[...] = st * decay + lax.dot_general(ks, v, _TN, preferred_element_type=F32)

    ms = jnp.mean(o * o, axis=-1, keepdims=True)
    on = o * lax.rsqrt(ms + EPS)
    o_ref[...] = (on * hg_ref[...] * _silu(gate_ref[...].astype(F32))).astype(BF16)


def _gla_recurrence(proj, la, head_g, *, batch, seq, dk_total, dv_total):
    m = proj.shape[0]
    hk = dk_total // GLA_HEADS
    hv = dv_total // GLA_HEADS
    nt = seq // CHUNK
    k_off = dk_total // hk
    v_off = 2 * dk_total // hv
    g_off = (2 * dk_total + dv_total) // hv
    rows = lambda b, h, t: b * nt + t
    return pl.pallas_call(
        functools.partial(_gla_rec_kernel, q_scale=float(hk) ** -0.5),
        grid=(batch, GLA_HEADS, nt),
        in_specs=[
            pl.BlockSpec((CHUNK, hk), lambda b, h, t: (rows(b, h, t), h)),
            pl.BlockSpec((CHUNK, hk), lambda b, h, t: (rows(b, h, t), k_off + h)),
            pl.BlockSpec((CHUNK, hv), lambda b, h, t: (rows(b, h, t), v_off + h)),
            pl.BlockSpec((CHUNK, hv), lambda b, h, t: (rows(b, h, t), g_off + h)),
            pl.BlockSpec((CHUNK, hk), lambda b, h, t: (rows(b, h, t), h)),
            pl.BlockSpec((1, hv), lambda b, h, t: (0, h)),
        ],
        out_specs=pl.BlockSpec((CHUNK, hv), lambda b, h, t: (rows(b, h, t), h)),
        out_shape=jax.ShapeDtypeStruct((m, dv_total), BF16),
        scratch_shapes=[
            pltpu.VMEM((hk, hv), F32),
            pltpu.VMEM((CHUNK, CHUNK), BF16),
            pltpu.VMEM((CHUNK, hk), F32),
            pltpu.VMEM((CHUNK, hk), F32),
            pltpu.VMEM((CHUNK, CHUNK), F32),
        ],
        compiler_params=pltpu.CompilerParams(
            dimension_semantics=("arbitrary", "arbitrary", "arbitrary"), vmem_limit_bytes=VMEM_LIMIT_BYTES),
        name="gla_recurrence",
    )(proj, proj, proj, proj, la, head_g)


def _out_proj_kernel(a_ref, w_ref, res_ref, o_ref):
    o_ref[...] = res_ref[...] + _dot(a_ref[...], w_ref[...])


def _out_proj(a, w, res, *, tm, tn):
    m, k = a.shape
    n = w.shape[1]
    return pl.pallas_call(
        _out_proj_kernel,
        grid=(m // tm, n // tn),
        in_specs=[
            pl.BlockSpec((tm, k), lambda i, j: (i, 0)),
            pl.BlockSpec((k, tn), lambda i, j: (0, j)),
            pl.BlockSpec((tm, tn), lambda i, j: (i, j)),
        ],
        out_specs=pl.BlockSpec((tm, tn), lambda i, j: (i, j)),
        out_shape=jax.ShapeDtypeStruct((m, n), F32),
        compiler_params=pltpu.CompilerParams(
            dimension_semantics=("arbitrary", "arbitrary"), vmem_limit_bytes=VMEM_LIMIT_BYTES),
        name="out_proj",
    )(a, w, res)


def _ret_in_kernel(x_ref, g_ref, pos_ref, invf_ref, w_ref, proj_ref, hn_ref, cos_ref, sin_ref,
                   *, n_q_blocks, n_k_blocks, hk, k_scale):
    j = pl.program_id(1)

    @pl.when(j == 0)
    def _():
        hn_ref[...] = _rms_rows(x_ref[...], g_ref[...]).astype(BF16)
        ang = pos_ref[...].astype(F32) * invf_ref[...]
        cos_ref[...] = jnp.cos(ang)
        sin_ref[...] = jnp.sin(ang)

    acc = _dot(hn_ref[...], w_ref[...])
    half = hk // 2

    @pl.when(j < n_q_blocks + n_k_blocks)
    def _():
        scale = jnp.where(j >= n_q_blocks, k_scale, 1.0).astype(F32)
        cos = cos_ref[...] * scale
        sin = sin_ref[...] * scale
        for h in range(acc.shape[1] // hk):
            t1 = acc[:, h * hk:h * hk + half]
            t2 = acc[:, h * hk + half:(h + 1) * hk]
            proj_ref[:, h * hk:h * hk + half] = (t1 * cos - t2 * sin).astype(BF16)
            proj_ref[:, h * hk + half:(h + 1) * hk] = (t2 * cos + t1 * sin).astype(BF16)

    @pl.when(j >= n_q_blocks + n_k_blocks)
    def _():
        proj_ref[...] = acc.astype(BF16)


def _ret_in_proj(x2, g, pos, inv_freq, w, *, dk_total, tm, tn):
    m, d = x2.shape
    n = w.shape[1]
    hk = dk_total // RET_HEADS
    half = hk // 2
    kern = functools.partial(_ret_in_kernel, n_q_blocks=dk_total // tn, n_k_blocks=dk_total // tn,
                             hk=hk, k_scale=float(hk) ** -0.5)
    return pl.pallas_call(
        kern,
        grid=(m // tm, n // tn),
        in_specs=[
            pl.BlockSpec((tm, d), lambda i, j: (i, 0)),
            pl.BlockSpec((1, d), lambda i, j: (0, 0)),
            pl.BlockSpec((tm, 1), lambda i, j: (i, 0)),
            pl.BlockSpec((1, half), lambda i, j: (0, 0)),
            pl.BlockSpec((d, tn), lambda i, j: (0, j)),
        ],
        out_specs=pl.BlockSpec((tm, tn), lambda i, j: (i, j)),
        out_shape=jax.ShapeDtypeStruct((m, n), BF16),
        scratch_shapes=[
            pltpu.VMEM((tm, d), BF16),
            pltpu.VMEM((tm, half), F32),
            pltpu.VMEM((tm, half), F32),
        ],
        compiler_params=pltpu.CompilerParams(
            dimension_semantics=("arbitrary", "arbitrary"), vmem_limit_bytes=VMEM_LIMIT_BYTES),
        name="ret_in_proj",
    )(x2, g, pos, inv_freq, w)


def _ret_rec_kernel(q_ref, k_ref, v_ref, gate_ref, lg_ref, gng_ref, gnb_ref, o_ref,
                    state_ref, dmat_ref, xi_ref, zeta_ref):
    c, dk = q_ref.shape
    dv = v_ref.shape[1]

    @pl.when(pl.program_id(2) == 0)
    def _():
        state_ref[...] = jnp.zeros_like(state_ref)
        row = lax.broadcasted_iota(jnp.int32, (c, c), 0)
        col = lax.broadcasted_iota(jnp.int32, (c, c), 1)
        dpos = (row - col).astype(F32)
        dmat_ref[...] = jnp.where(dpos >= 0, jnp.exp(lg_ref[:, :c] * jnp.maximum(dpos, 0.0)), 0.0)
        idx_v = lax.broadcasted_iota(jnp.int32, (c, dv), 0).astype(F32)
        xi_ref[...] = jnp.exp(lg_ref[:, :dv] * (idx_v + 1.0))
        idx_k = lax.broadcasted_iota(jnp.int32, (c, dk), 0).astype(F32)
        zeta_ref[...] = jnp.exp(lg_ref[:, :dk] * (c - 1.0 - idx_k))

    q = q_ref[...]
    k = k_ref[...]
    v = v_ref[...]
    s = lax.dot_general(q, k, _NT, preferred_element_type=F32) * dmat_ref[...]
    st = state_ref[...]
    o = _dot(s.astype(BF16), v) + _dot(q, st.astype(BF16)) * xi_ref[...]
    kz = (k.astype(F32) * zeta_ref[...]).astype(BF16)
    state_ref[...] = st * jnp.exp(lg_ref[:, :dv] * float(c)) + lax.dot_general(kz, v, _TN, preferred_element_type=F32)

    mu = jnp.mean(o, axis=-1, keepdims=True)
    d = o - mu
    var = jnp.mean(d * d, axis=-1, keepdims=True)
    on = d * lax.rsqrt(var + EPS)
    o_ref[...] = ((on * gng_ref[...] + gnb_ref[...]) * _silu(gate_ref[...].astype(F32))).astype(BF16)


def _ret_recurrence(proj, log_gamma, gn_g, gn_b, *, batch, seq, dk_total, dv_total):
    m = proj.shape[0]
    hk = dk_total // RET_HEADS
    hv = dv_total // RET_HEADS
    nt = seq // CHUNK
    width = log_gamma.shape[-1]
    k_off = dk_total // hk
    v_off = 2 * dk_total // hv
    g_off = (2 * dk_total + dv_total) // hv
    rows = lambda b, h, t: b * nt + t
    return pl.pallas_call(
        _ret_rec_kernel,
        grid=(batch, RET_HEADS, nt),
        in_specs=[
            pl.BlockSpec((CHUNK, hk), lambda b, h, t: (rows(b, h, t), h)),
            pl.BlockSpec((CHUNK, hk), lambda b, h, t: (rows(b, h, t), k_off + h)),
            pl.BlockSpec((CHUNK, hv), lambda b, h, t: (rows(b, h, t), v_off + h)),
            pl.BlockSpec((CHUNK, hv), lambda b, h, t: (rows(b, h, t), g_off + h)),
            pl.BlockSpec((None, 1, width), lambda b, h, t: (h, 0, 0)),
            pl.BlockSpec((1, hv), lambda b, h, t: (0, h)),
            pl.BlockSpec((1, hv), lambda b, h, t: (0, h)),
        ],
        out_specs=pl.BlockSpec((CHUNK, hv), lambda b, h, t: (rows(b, h, t), h)),
        out_shape=jax.ShapeDtypeStruct((m, dv_total), BF16),
        scratch_shapes=[
            pltpu.VMEM((hk, hv), F32),
            pltpu.VMEM((CHUNK, CHUNK), F32),
            pltpu.VMEM((CHUNK, hv), F32),
            pltpu.VMEM((CHUNK, hk), F32),
        ],
        compiler_params=pltpu.CompilerParams(
            dimension_semantics=("arbitrary", "arbitrary", "arbitrary"), vmem_limit_bytes=VMEM_LIMIT_BYTES),
        name="ret_recurrence",
    )(proj, proj, proj, proj, log_gamma, gn_g, gn_b)


def _rms_kernel(x_ref, g_ref, o_ref):
    o_ref[...] = _rms_rows(x_ref[...], g_ref[...])


def _rms_norm(x2, g, *, tm):
    m, d = x2.shape
    return pl.pallas_call(
        _rms_kernel,
        grid=(m // tm,),
        in_specs=[pl.BlockSpec((tm, d), lambda i: (i, 0)), pl.BlockSpec((1, d), lambda i: (0, 0))],
        out_specs=pl.BlockSpec((tm, d), lambda i: (i, 0)),
        out_shape=jax.ShapeDtypeStruct((m, d), F32),
        compiler_params=pltpu.CompilerParams(dimension_semantics=("arbitrary",)),
        name="final_rms_norm",
    )(x2, g)


def _gla_layer(h, batch, seq, norm_g, w_in, w_a1, w_a2, b_a, head_g, w_out):
    d = h.shape[1]
    dk_total = w_a2.shape[1]
    dv_total = w_out.shape[0]
    rank = w_a1.shape[1]
    wa1 = jnp.zeros((d, GLA_GATE_RANK_PAD), BF16).at[:, :rank].set(w_a1.astype(BF16))
    wa2 = jnp.zeros((GLA_GATE_RANK_PAD, dk_total), BF16).at[:rank, :].set(w_a2.astype(BF16))
    proj, la = _gla_in_proj(h, norm_g[None, :], w_in.astype(BF16), wa1, wa2, b_a[None, :], tm=512, tn=1024)
    o = _gla_recurrence(proj, la, head_g[None, :], batch=batch, seq=seq, dk_total=dk_total, dv_total=dv_total)
    return _out_proj(o, w_out.astype(BF16), h, tm=1024, tn=512)


def _ret_layer(h, pos, batch, seq, norm_g, w_in, gn_g, gn_b, w_out):
    dv_total = w_out.shape[0]
    dk_total = (w_in.shape[1] - 2 * dv_total) // 2
    hk = dk_total // RET_HEADS
    half = hk // 2
    inv_freq = (ROPE_BASE ** (-jnp.arange(half, dtype=F32) / half))[None, :]
    log_gamma = jnp.log1p(-jnp.exp2(-5.0 - jnp.arange(RET_HEADS, dtype=F32)))
    width = max(CHUNK, hk, dv_total // RET_HEADS)
    log_gamma = jnp.broadcast_to(log_gamma[:, None, None], (RET_HEADS, 1, width))
    proj = _ret_in_proj(h, norm_g[None, :], pos, inv_freq, w_in.astype(BF16), dk_total=dk_total, tm=512, tn=1024)
    o = _ret_recurrence(proj, log_gamma, gn_g[None, :], gn_b[None, :],
                        batch=batch, seq=seq, dk_total=dk_total, dv_total=dv_total)
    return _out_proj(o, w_out.astype(BF16), h, tm=1024, tn=512)


def kernel(x, positions, gla_norm, gla_w_in, gla_w_a1, gla_w_a2, gla_b_a, gla_head_g, gla_w_out,
           ret_norm, ret_w_in, ret_gn_g, ret_gn_b, ret_w_out, final_norm):
    batch, seq, d = x.shape
    m = batch * seq
    h = x.reshape(m, d)
    pos = positions.reshape(m, 1)
    for i in range(gla_norm.shape[0] + ret_norm.shape[0]):
        j = i // 2
        if i % 2 == 0:
            h = _gla_layer(h, batch, seq, gla_norm[j], gla_w_in[j], gla_w_a1[j], gla_w_a2[j],
                           gla_b_a[j], gla_head_g[j], gla_w_out[j])
        else:
            h = _ret_layer(h, pos, batch, seq, ret_norm[j], ret_w_in[j], ret_gn_g[j], ret_gn_b[j], ret_w_out[j])
    return _rms_norm(h, final_norm[None, :], tm=512).reshape(batch, seq, d)
```

```python
import functools

import jax
import jax.numpy as jnp
from jax import lax
from jax.experimental import pallas as pl
from jax.experimental.pallas import tpu as pltpu

F32 = jnp.float32
BF16 = jnp.bfloat16

EPS = 1e-6
GLA_HEADS = 4
GLA_GATE_RANK_PAD = 128
GLA_GATE_TEMP = 16.0
RET_HEADS = 8
ROPE_BASE = 10000.0

CHUNK = 256
CHUNKS_PER_STEP = 4
FAST_DECAY_LIMIT = 60.0
VMEM_LIMIT_BYTES = 56 * 1024 * 1024

_NT = (((1,), (1,)), ((), ()))
_TN = (((0,), (0,)), ((), ()))


def _dot(a, b):
    return jnp.dot(a, b, preferred_element_type=F32)


def _rms_rows(x, g):
    ms = jnp.mean(x * x, axis=-1, keepdims=True)
    return x * lax.rsqrt(ms + EPS) * g


def _silu(g):
    hg = 0.5 * g
    return hg + hg * jnp.tanh(hg)


def _split3(a):
    hi = a.astype(BF16)
    r1 = a - hi.astype(F32)
    mid = r1.astype(BF16)
    lo = (r1 - mid.astype(F32)).astype(BF16)
    return hi, mid, lo


def _causal(c):
    row = lax.broadcasted_iota(jnp.int32, (c, c), 0)
    col = lax.broadcasted_iota(jnp.int32, (c, c), 1)
    return row >= col


def _gla_in_kernel(x_ref, g_ref, w_ref, wa1_ref, wa2_ref, ba_ref, proj_ref, la_ref, hn_ref):
    @pl.when(pl.program_id(1) == 0)
    def _():
        hn = _rms_rows(x_ref[...], g_ref[...]).astype(BF16)
        hn_ref[...] = hn
        z = _dot(hn, wa1_ref[...])
        xa = _dot(z.astype(BF16), wa2_ref[...]) + ba_ref[...]
        la_ref[...] = (jnp.minimum(xa, 0.0) - jnp.log1p(jnp.exp(-jnp.abs(xa)))) * (1.0 / GLA_GATE_TEMP)

    proj_ref[...] = _dot(hn_ref[...], w_ref[...]).astype(BF16)


def _gla_in_proj(x2, g, w, wa1, wa2, ba, *, tm, tn):
    m, d = x2.shape
    n = w.shape[1]
    dk = wa2.shape[1]
    return pl.pallas_call(
        _gla_in_kernel,
        grid=(m // tm, n // tn),
        in_specs=[
            pl.BlockSpec((tm, d), lambda i, j: (i, 0)),
            pl.BlockSpec((1, d), lambda i, j: (0, 0)),
            pl.BlockSpec((d, tn), lambda i, j: (0, j)),
            pl.BlockSpec((d, GLA_GATE_RANK_PAD), lambda i, j: (0, 0)),
            pl.BlockSpec((GLA_GATE_RANK_PAD, dk), lambda i, j: (0, 0)),
            pl.BlockSpec((1, dk), lambda i, j: (0, 0)),
        ],
        out_specs=[
            pl.BlockSpec((tm, tn), lambda i, j: (i, j)),
            pl.BlockSpec((tm, dk), lambda i, j: (i, 0)),
        ],
        out_shape=[
            jax.ShapeDtypeStruct((m, n), BF16),
            jax.ShapeDtypeStruct((m, dk), F32),
        ],
        scratch_shapes=[pltpu.VMEM((tm, d), BF16)],
        compiler_params=pltpu.CompilerParams(
            dimension_semantics=("arbitrary", "arbitrary"), vmem_limit_bytes=VMEM_LIMIT_BYTES),
        name="gla_in_proj",
    )(x2, g, w, wa1, wa2, ba)


def _gla_rec_kernel(q_ref, k_ref, v_ref, gate_ref, la_ref, hg_ref, o_ref,
                    state_ref, ltri_ref, p_ref, cum_ref, kf_ref, sc_ref, *, q_scale):
    c = CHUNK
    dv = v_ref.shape[1]

    @pl.when(pl.program_id(2) == 0)
    def _():
        state_ref[...] = jnp.zeros_like(state_ref)
        ltri_ref[...] = jnp.where(_causal(c), 1.0, 0.0).astype(BF16)

    ones = jnp.ones((c, 128), BF16)

    for i in range(q_ref.shape[0] // c):
        rows = pl.ds(i * c, c)
        ltri = ltri_ref[...]
        a_hi, a_mid, a_lo = _split3(la_ref[rows, :])
        cum = _dot(ltri, a_hi) + _dot(ltri, a_mid) + _dot(ltri, a_lo)
        tot_col = (lax.dot_general(a_hi, ones, _TN, preferred_element_type=F32)
                   + lax.dot_general(a_mid, ones, _TN, preferred_element_type=F32)
                   + lax.dot_general(a_lo, ones, _TN, preferred_element_type=F32))
        total = cum[c - 1:c, :]

        qf = q_ref[rows, :].astype(F32) * q_scale
        kf = k_ref[rows, :].astype(F32)
        v = v_ref[rows, :]
        qs = (qf * jnp.exp(cum)).astype(BF16)
        ks = (kf * jnp.exp(total - cum)).astype(BF16)

        fast = jnp.min(total) >= -FAST_DECAY_LIMIT

        @pl.when(fast)
        def _():
            kn = (kf * jnp.exp(-cum)).astype(BF16)
            s = lax.dot_general(qs, kn, _NT, preferred_element_type=F32)
            p_ref[...] = jnp.where(_causal(c), s, 0.0).astype(BF16)

        @pl.when(jnp.logical_not(fast))
        def _():
            cum_ref[...] = cum
            kf_ref[...] = kf
            sc_ref[...] = jnp.zeros_like(sc_ref)
            lane = lax.broadcasted_iota(jnp.int32, (1, c), 1)

            def column(s, carry):
                d = jnp.exp(jnp.minimum(cum_ref[...] - cum_ref[pl.ds(s, 1), :], 0.0))
                colv = jnp.sum(qf * kf_ref[pl.ds(s, 1), :] * d, axis=1, keepdims=True)
                sc_ref[...] += colv * jnp.where(lane == s, 1.0, 0.0)
                return carry

            lax.fori_loop(0, c, column, 0)
            p_ref[...] = jnp.where(_causal(c), sc_ref[...], 0.0).astype(BF16)

        st = state_ref[...]
        o = _dot(qs, st.astype(BF16)) + _dot(p_ref[...], v)
        decay = jnp.exp(tot_col)
        decay = jnp.concatenate([decay] * (dv // 128), axis=1)
        state_ref[...] = st * decay + lax.dot_general(ks, v, _TN, preferred_element_type=F32)

        ms = jnp.mean(o * o, axis=-1, keepdims=True)
        on = o * lax.rsqrt(ms + EPS)
        o_ref[rows, :] = (on * hg_ref[...] * _silu(gate_ref[rows, :].astype(F32))).astype(BF16)


def _gla_recurrence(proj, la, head_g, *, batch, seq, dk_total, dv_total):
    m = proj.shape[0]
    hk = dk_total // GLA_HEADS
    hv = dv_total // GLA_HEADS
    t_rows = CHUNK * CHUNKS_PER_STEP
    nt = seq // t_rows
    k_off = dk_total // hk
    v_off = 2 * dk_total // hv
    g_off = (2 * dk_total + dv_total) // hv
    rows = lambda b, h, t: b * nt + t
    return pl.pallas_call(
        functools.partial(_gla_rec_kernel, q_scale=float(hk) ** -0.5),
        grid=(batch, GLA_HEADS, nt),
        in_specs=[
            pl.BlockSpec((t_rows, hk), lambda b, h, t: (rows(b, h, t), h)),
            pl.BlockSpec((t_rows, hk), lambda b, h, t: (rows(b, h, t), k_off + h)),
            pl.BlockSpec((t_rows, hv), lambda b, h, t: (rows(b, h, t), v_off + h)),
            pl.BlockSpec((t_rows, hv), lambda b, h, t: (rows(b, h, t), g_off + h)),
            pl.BlockSpec((t_rows, hk), lambda b, h, t: (rows(b, h, t), h)),
            pl.BlockSpec((1, hv), lambda b, h, t: (0, h)),
        ],
        out_specs=pl.BlockSpec((t_rows, hv), lambda b, h, t: (rows(b, h, t), h)),
        out_shape=jax.ShapeDtypeStruct((m, dv_total), BF16),
        scratch_shapes=[
            pltpu.VMEM((hk, hv), F32),
            pltpu.VMEM((CHUNK, CHUNK), BF16),
            pltpu.VMEM((CHUNK, CHUNK), BF16),
            pltpu.VMEM((CHUNK, hk), F32),
            pltpu.VMEM((CHUNK, hk), F32),
            pltpu.VMEM((CHUNK, CHUNK), F32),
        ],
        compiler_params=pltpu.CompilerParams(
            dimension_semantics=("arbitrary", "arbitrary", "arbitrary"), vmem_limit_bytes=VMEM_LIMIT_BYTES),
        name="gla_recurrence",
    )(proj, proj, proj, proj, la, head_g)


def _out_proj_kernel(a_ref, w_ref, res_ref, o_ref):
    o_ref[...] = res_ref[...] + _dot(a_ref[...], w_ref[...])


def _out_proj(a, w, res, *, tm, tn):
    m, k = a.shape
    n = w.shape[1]
    return pl.pallas_call(
        _out_proj_kernel,
        grid=(m // tm, n // tn),
        in_specs=[
            pl.BlockSpec((tm, k), lambda i, j: (i, 0)),
            pl.BlockSpec((k, tn), lambda i, j: (0, j)),
            pl.BlockSpec((tm, tn), lambda i, j: (i, j)),
        ],
        out_specs=pl.BlockSpec((tm, tn), lambda i, j: (i, j)),
        out_shape=jax.ShapeDtypeStruct((m, n), F32),
        compiler_params=pltpu.CompilerParams(
            dimension_semantics=("arbitrary", "arbitrary"), vmem_limit_bytes=VMEM_LIMIT_BYTES),
        name="out_proj",
    )(a, w, res)


def _ret_in_kernel(x_ref, g_ref, pos_ref, invf_ref, w_ref, proj_ref, hn_ref, cos_ref, sin_ref,
                   *, n_q_blocks, n_k_blocks, hk, k_scale):
    j = pl.program_id(1)

    @pl.when(j == 0)
    def _():
        hn_ref[...] = _rms_rows(x_ref[...], g_ref[...]).astype(BF16)
        ang = pos_ref[...].astype(F32) * invf_ref[...]
        cos_ref[...] = jnp.cos(ang)
        sin_ref[...] = jnp.sin(ang)

    acc = _dot(hn_ref[...], w_ref[...])
    half = hk // 2

    @pl.when(j < n_q_blocks + n_k_blocks)
    def _():
        scale = jnp.where(j >= n_q_blocks, k_scale, 1.0).astype(F32)
        cos = cos_ref[...] * scale
        sin = sin_ref[...] * scale
        for h in range(acc.shape[1] // hk):
            t1 = acc[:, h * hk:h * hk + half]
            t2 = acc[:, h * hk + half:(h + 1) * hk]
            proj_ref[:, h * hk:h * hk + half] = (t1 * cos - t2 * sin).astype(BF16)
            proj_ref[:, h * hk + half:(h + 1) * hk] = (t2 * cos + t1 * sin).astype(BF16)

    @pl.when(j >= n_q_blocks + n_k_blocks)
    def _():
        proj_ref[...] = acc.astype(BF16)


def _ret_in_proj(x2, g, pos, inv_freq, w, *, dk_total, tm, tn):
    m, d = x2.shape
    n = w.shape[1]
    hk = dk_total // RET_HEADS
    half = hk // 2
    kern = functools.partial(_ret_in_kernel, n_q_blocks=dk_total // tn, n_k_blocks=dk_total // tn,
                             hk=hk, k_scale=float(hk) ** -0.5)
    return pl.pallas_call(
        kern,
        grid=(m // tm, n // tn),
        in_specs=[
            pl.BlockSpec((tm, d), lambda i, j: (i, 0)),
            pl.BlockSpec((1, d), lambda i, j: (0, 0)),
            pl.BlockSpec((tm, 1), lambda i, j: (i, 0)),
            pl.BlockSpec((1, half), lambda i, j: (0, 0)),
            pl.BlockSpec((d, tn), lambda i, j: (0, j)),
        ],
        out_specs=pl.BlockSpec((tm, tn), lambda i, j: (i, j)),
        out_shape=jax.ShapeDtypeStruct((m, n), BF16),
        scratch_shapes=[
            pltpu.VMEM((tm, d), BF16),
            pltpu.VMEM((tm, half), F32),
            pltpu.VMEM((tm, half), F32),
        ],
        compiler_params=pltpu.CompilerParams(
            dimension_semantics=("arbitrary", "arbitrary"), vmem_limit_bytes=VMEM_LIMIT_BYTES),
        name="ret_in_proj",
    )(x2, g, pos, inv_freq, w)


def _ret_rec_kernel(q_ref, k_ref, v_ref, gate_ref, lg_ref, gng_ref, gnb_ref, o_ref,
                    state_ref, dmat_ref, xi_ref, zeta_ref):
    c = CHUNK
    dk = q_ref.shape[1]
    dv = v_ref.shape[1]

    @pl.when(pl.program_id(2) == 0)
    def _():
        state_ref[...] = jnp.zeros_like(state_ref)
        row = lax.broadcasted_iota(jnp.int32, (c, c), 0)
        col = lax.broadcasted_iota(jnp.int32, (c, c), 1)
        dpos = (row - col).astype(F32)
        dmat_ref[...] = jnp.where(dpos >= 0, jnp.exp(lg_ref[:, :c] * jnp.maximum(dpos, 0.0)), 0.0)
        idx_v = lax.broadcasted_iota(jnp.int32, (c, dv), 0).astype(F32)
        xi_ref[...] = jnp.exp(lg_ref[:, :dv] * (idx_v + 1.0))
        idx_k = lax.broadcasted_iota(jnp.int32, (c, dk), 0).astype(F32)
        zeta_ref[...] = jnp.exp(lg_ref[:, :dk] * (c - 1.0 - idx_k))

    for i in range(q_ref.shape[0] // c):
        rows = pl.ds(i * c, c)
        q = q_ref[rows, :]
        k = k_ref[rows, :]
        v = v_ref[rows, :]
        s = lax.dot_general(q, k, _NT, preferred_element_type=F32) * dmat_ref[...]
        st = state_ref[...]
        o = _dot(s.astype(BF16), v) + _dot(q, st.astype(BF16)) * xi_ref[...]
        kz = (k.astype(F32) * zeta_ref[...]).astype(BF16)
        state_ref[...] = (st * jnp.exp(lg_ref[:, :dv] * float(c))
                          + lax.dot_general(kz, v, _TN, preferred_element_type=F32))

        mu = jnp.mean(o, axis=-1, keepdims=True)
        d = o - mu
        var = jnp.mean(d * d, axis=-1, keepdims=True)
        on = d * lax.rsqrt(var + EPS)
        o_ref[rows, :] = ((on * gng_ref[...] + gnb_ref[...]) * _silu(gate_ref[rows, :].astype(F32))).astype(BF16)


def _ret_recurrence(proj, log_gamma, gn_g, gn_b, *, batch, seq, dk_total, dv_total):
    m = proj.shape[0]
    hk = dk_total // RET_HEADS
    hv = dv_total // RET_HEADS
    t_rows = CHUNK * CHUNKS_PER_STEP
    nt = seq // t_rows
    width = log_gamma.shape[-1]
    k_off = dk_total // hk
    v_off = 2 * dk_total // hv
    g_off = (2 * dk_total + dv_total) // hv
    rows = lambda b, h, t: b * nt + t
    return pl.pallas_call(
        _ret_rec_kernel,
        grid=(batch, RET_HEADS, nt),
        in_specs=[
            pl.BlockSpec((t_rows, hk), lambda b, h, t: (rows(b, h, t), h)),
            pl.BlockSpec((t_rows, hk), lambda b, h, t: (rows(b, h, t), k_off + h)),
            pl.BlockSpec((t_rows, hv), lambda b, h, t: (rows(b, h, t), v_off + h)),
            pl.BlockSpec((t_rows, hv), lambda b, h, t: (rows(b, h, t), g_off + h)),
            pl.BlockSpec((None, 1, width), lambda b, h, t: (h, 0, 0)),
            pl.BlockSpec((1, hv), lambda b, h, t: (0, h)),
            pl.BlockSpec((1, hv), lambda b, h, t: (0, h)),
        ],
        out_specs=pl.BlockSpec((t_rows, hv), lambda b, h, t: (rows(b, h, t), h)),
        out_shape=jax.ShapeDtypeStruct((m, dv_total), BF16),
        scratch_shapes=[
            pltpu.VMEM((hk, hv), F32),
            pltpu.VMEM((CHUNK, CHUNK), F32),
            pltpu.VMEM((CHUNK, hv), F32),
            pltpu.VMEM((CHUNK, hk), F32),
        ],
        compiler_params=pltpu.CompilerParams(
            dimension_semantics=("arbitrary", "arbitrary", "arbitrary"), vmem_limit_bytes=VMEM_LIMIT_BYTES),
        name="ret_recurrence",
    )(proj, proj, proj, proj, log_gamma, gn_g, gn_b)


def _rms_kernel(x_ref, g_ref, o_ref):
    o_ref[...] = _rms_rows(x_ref[...], g_ref[...])


def _rms_norm(x2, g, *, tm):
    m, d = x2.shape
    return pl.pallas_call(
        _rms_kernel,
        grid=(m // tm,),
        in_specs=[pl.BlockSpec((tm, d), lambda i: (i, 0)), pl.BlockSpec((1, d), lambda i: (0, 0))],
        out_specs=pl.BlockSpec((tm, d), lambda i: (i, 0)),
        out_shape=jax.ShapeDtypeStruct((m, d), F32),
        compiler_params=pltpu.CompilerParams(dimension_semantics=("arbitrary",)),
        name="final_rms_norm",
    )(x2, g)


def _gla_layer(h, batch, seq, norm_g, w_in, w_a1, w_a2, b_a, head_g, w_out):
    d = h.shape[1]
    dk_total = w_a2.shape[1]
    dv_total = w_out.shape[0]
    rank = w_a1.shape[1]
    wa1 = jnp.zeros((d, GLA_GATE_RANK_PAD), BF16).at[:, :rank].set(w_a1.astype(BF16))
    wa2 = jnp.zeros((GLA_GATE_RANK_PAD, dk_total), BF16).at[:rank, :].set(w_a2.astype(BF16))
    proj, la = _gla_in_proj(h, norm_g[None, :], w_in.astype(BF16), wa1, wa2, b_a[None, :], tm=512, tn=1024)
    o = _gla_recurrence(proj, la, head_g[None, :], batch=batch, seq=seq, dk_total=dk_total, dv_total=dv_total)
    return _out_proj(o, w_out.astype(BF16), h, tm=1024, tn=512)


def _ret_layer(h, pos, batch, seq, norm_g, w_in, gn_g, gn_b, w_out):
    dv_total = w_out.shape[0]
    dk_total = (w_in.shape[1] - 2 * dv_total) // 2
    hk = dk_total // RET_HEADS
    half = hk // 2
    inv_freq = (ROPE_BASE ** (-jnp.arange(half, dtype=F32) / half))[None, :]
    log_gamma = jnp.log1p(-jnp.exp2(-5.0 - jnp.arange(RET_HEADS, dtype=F32)))
    width = max(CHUNK, hk, dv_total // RET_HEADS)
    log_gamma = jnp.broadcast_to(log_gamma[:, None, None], (RET_HEADS, 1, width))
    proj = _ret_in_proj(h, norm_g[None, :], pos, inv_freq, w_in.astype(BF16), dk_total=dk_total, tm=512, tn=1024)
    o = _ret_recurrence(proj, log_gamma, gn_g[None, :], gn_b[None, :],
                        batch=batch, seq=seq, dk_total=dk_total, dv_total=dv_total)
    return _out_proj(o, w_out.astype(BF16), h, tm=1024, tn=512)


def kernel(x, positions, gla_norm, gla_w_in, gla_w_a1, gla_w_a2, gla_b_a, gla_head_g, gla_w_out,
           ret_norm, ret_w_in, ret_gn_g, ret_gn_b, ret_w_out, final_norm):
    batch, seq, d = x.shape
    m = batch * seq
    h = x.reshape(m, d)
    pos = positions.reshape(m, 1)
    for i in range(gla_norm.shape[0] + ret_norm.shape[0]):
        j = i // 2
        if i % 2 == 0:
            h = _gla_layer(h, batch, seq, gla_norm[j], gla_w_in[j], gla_w_a1[j], gla_w_a2[j],
                           gla_b_a[j], gla_head_g[j], gla_w_out[j])
        else:
            h = _ret_layer(h, pos, batch, seq, ret_norm[j], ret_w_in[j], ret_gn_g[j], ret_gn_b[j], ret_w_out[j])
    return _rms_norm(h, final_norm[None, :], tm=512).reshape(batch, seq, d)
```

```python
import functools

import jax
import jax.numpy as jnp
from jax import lax
from jax.experimental import pallas as pl
from jax.experimental.pallas import tpu as pltpu

F32 = jnp.float32
BF16 = jnp.bfloat16

EPS = 1e-6
GLA_HEADS = 4
GLA_GATE_RANK_PAD = 128
GLA_GATE_TEMP = 16.0
RET_HEADS = 8
ROPE_BASE = 10000.0

CHUNK = 256
CHUNKS_PER_STEP = 4
FAST_DECAY_LIMIT = 60.0
VMEM_LIMIT_BYTES = 56 * 1024 * 1024

_NT = (((1,), (1,)), ((), ()))
_TN = (((0,), (0,)), ((), ()))


def _dot(a, b):
    return jnp.dot(a, b, preferred_element_type=F32)


def _rms_rows(x, g):
    ms = jnp.mean(x * x, axis=-1, keepdims=True)
    return x * lax.rsqrt(ms + EPS) * g


def _silu(g):
    hg = 0.5 * g
    return hg + hg * jnp.tanh(hg)


def _split3(a):
    hi = a.astype(BF16)
    r1 = a - hi.astype(F32)
    mid = r1.astype(BF16)
    lo = (r1 - mid.astype(F32)).astype(BF16)
    return hi, mid, lo


def _causal(c):
    row = lax.broadcasted_iota(jnp.int32, (c, c), 0)
    col = lax.broadcasted_iota(jnp.int32, (c, c), 1)
    return row >= col


def _gla_in_kernel(x_ref, g_ref, w_ref, wa1_ref, wa2_ref, ba_ref, proj_ref, la_ref, hn_ref):
    @pl.when(pl.program_id(1) == 0)
    def _():
        hn = _rms_rows(x_ref[...], g_ref[...]).astype(BF16)
        hn_ref[...] = hn
        z = _dot(hn, wa1_ref[...])
        xa = _dot(z.astype(BF16), wa2_ref[...]) + ba_ref[...]
        la_ref[...] = (jnp.minimum(xa, 0.0) - jnp.log1p(jnp.exp(-jnp.abs(xa)))) * (1.0 / GLA_GATE_TEMP)

    proj_ref[...] = _dot(hn_ref[...], w_ref[...]).astype(BF16)


def _gla_in_proj(x2, g, w, wa1, wa2, ba, *, tm, tn):
    m, d = x2.shape
    n = w.shape[1]
    dk = wa2.shape[1]
    return pl.pallas_call(
        _gla_in_kernel,
        grid=(m // tm, n // tn),
        in_specs=[
            pl.BlockSpec((tm, d), lambda i, j: (i, 0)),
            pl.BlockSpec((1, d), lambda i, j: (0, 0)),
            pl.BlockSpec((d, tn), lambda i, j: (0, j)),
            pl.BlockSpec((d, GLA_GATE_RANK_PAD), lambda i, j: (0, 0)),
            pl.BlockSpec((GLA_GATE_RANK_PAD, dk), lambda i, j: (0, 0)),
            pl.BlockSpec((1, dk), lambda i, j: (0, 0)),
        ],
        out_specs=[
            pl.BlockSpec((tm, tn), lambda i, j: (i, j)),
            pl.BlockSpec((tm, dk), lambda i, j: (i, 0)),
        ],
        out_shape=[
            jax.ShapeDtypeStruct((m, n), BF16),
            jax.ShapeDtypeStruct((m, dk), F32),
        ],
        scratch_shapes=[pltpu.VMEM((tm, d), BF16)],
        compiler_params=pltpu.CompilerParams(
            dimension_semantics=("arbitrary", "arbitrary"), vmem_limit_bytes=VMEM_LIMIT_BYTES),
        name="gla_in_proj",
    )(x2, g, w, wa1, wa2, ba)


def _gla_rec_kernel(q_ref, k_ref, v_ref, gate_ref, la_ref, hg_ref, o_ref,
                    state_ref, ltri_ref, p_ref, cum_ref, kf_ref, sc_ref, *, q_scale):
    c = CHUNK
    dv = v_ref.shape[1]

    @pl.when(pl.program_id(2) == 0)
    def _():
        state_ref[...] = jnp.zeros_like(state_ref)
        ltri_ref[...] = jnp.where(_causal(c), 1.0, 0.0).astype(BF16)

    ones = jnp.ones((c, 128), BF16)

    for i in range(q_ref.shape[0] // c):
        rows = pl.ds(i * c, c)
        ltri = ltri_ref[...]
        a_hi, a_mid, a_lo = _split3(la_ref[rows, :])
        cum = _dot(ltri, a_hi) + _dot(ltri, a_mid) + _dot(ltri, a_lo)
        tot_col = (lax.dot_general(a_hi, ones, _TN, preferred_element_type=F32)
                   + lax.dot_general(a_mid, ones, _TN, preferred_element_type=F32)
                   + lax.dot_general(a_lo, ones, _TN, preferred_element_type=F32))
        total = cum[c - 1:c, :]

        qf = q_ref[rows, :].astype(F32) * q_scale
        kf = k_ref[rows, :].astype(F32)
        v = v_ref[rows, :]
        qs = (qf * jnp.exp(cum)).astype(BF16)
        ks = (kf * jnp.exp(total - cum)).astype(BF16)

        fast = jnp.min(total) >= -FAST_DECAY_LIMIT

        @pl.when(fast)
        def _():
            kn = (kf * jnp.exp(-cum)).astype(BF16)
            s = lax.dot_general(qs, kn, _NT, preferred_element_type=F32)
            p_ref[...] = jnp.where(_causal(c), s, 0.0).astype(BF16)

        @pl.when(jnp.logical_not(fast))
        def _():
            cum_ref[...] = cum
            kf_ref[...] = kf
            sc_ref[...] = jnp.zeros_like(sc_ref)
            lane = lax.broadcasted_iota(jnp.int32, (1, c), 1)

            def column(s, carry):
                d = jnp.exp(jnp.minimum(cum_ref[...] - cum_ref[pl.ds(s, 1), :], 0.0))
                colv = jnp.sum(qf * kf_ref[pl.ds(s, 1), :] * d, axis=1, keepdims=True)
                sc_ref[...] += colv * jnp.where(lane == s, 1.0, 0.0)
                return carry

            lax.fori_loop(0, c, column, 0)
            p_ref[...] = jnp.where(_causal(c), sc_ref[...], 0.0).astype(BF16)

        st = state_ref[...]
        o = _dot(qs, st.astype(BF16)) + _dot(p_ref[...], v)
        decay = jnp.exp(tot_col)
        decay = jnp.concatenate([decay] * (dv // 128), axis=1)
        state_ref[...] = st * decay + lax.dot_general(ks, v, _TN, preferred_element_type=F32)

        ms = jnp.mean(o * o, axis=-1, keepdims=True)
        on = o * lax.rsqrt(ms + EPS)
        o_ref[rows, :] = (on * hg_ref[...] * _silu(gate_ref[rows, :].astype(F32))).astype(BF16)


def _gla_recurrence(proj, la, head_g, *, batch, seq, dk_total, dv_total):
    m = proj.shape[0]
    hk = dk_total // GLA_HEADS
    hv = dv_total // GLA_HEADS
    t_rows = CHUNK * CHUNKS_PER_STEP
    nt = seq // t_rows
    k_off = dk_total // hk
    v_off = 2 * dk_total // hv
    g_off = (2 * dk_total + dv_total) // hv
    rows = lambda b, h, t: b * nt + t
    return pl.pallas_call(
        functools.partial(_gla_rec_kernel, q_scale=float(hk) ** -0.5),
        grid=(batch, GLA_HEADS, nt),
        in_specs=[
            pl.BlockSpec((t_rows, hk), lambda b, h, t: (rows(b, h, t), h)),
            pl.BlockSpec((t_rows, hk), lambda b, h, t: (rows(b, h, t), k_off + h)),
            pl.BlockSpec((t_rows, hv), lambda b, h, t: (rows(b, h, t), v_off + h)),
            pl.BlockSpec((t_rows, hv), lambda b, h, t: (rows(b, h, t), g_off + h)),
            pl.BlockSpec((t_rows, hk), lambda b, h, t: (rows(b, h, t), h)),
            pl.BlockSpec((1, hv), lambda b, h, t: (0, h)),
        ],
        out_specs=pl.BlockSpec((t_rows, hv), lambda b, h, t: (rows(b, h, t), h)),
        out_shape=jax.ShapeDtypeStruct((m, dv_total), BF16),
        scratch_shapes=[
            pltpu.VMEM((hk, hv), F32),
            pltpu.VMEM((CHUNK, CHUNK), BF16),
            pltpu.VMEM((CHUNK, CHUNK), BF16),
            pltpu.VMEM((CHUNK, hk), F32),
            pltpu.VMEM((CHUNK, hk), F32),
            pltpu.VMEM((CHUNK, CHUNK), F32),
        ],
        compiler_params=pltpu.CompilerParams(
            dimension_semantics=("arbitrary", "arbitrary", "arbitrary"), vmem_limit_bytes=VMEM_LIMIT_BYTES),
        name="gla_recurrence",
    )(proj, proj, proj, proj, la, head_g)


def _out_proj_kernel(a_ref, w_ref, res_ref, o_ref):
    o_ref[...] = res_ref[...] + _dot(a_ref[...], w_ref[...])


def _out_proj_norm_kernel(a_ref, w_ref, res_ref, g_ref, o_ref):
    o_ref[...] = _rms_rows(res_ref[...] + _dot(a_ref[...], w_ref[...]), g_ref[...])


def _out_proj(a, w, res, norm_g=None, *, tm):
    m, k = a.shape
    n = w.shape[1]
    in_specs = [
        pl.BlockSpec((tm, k), lambda i: (i, 0)),
        pl.BlockSpec((k, n), lambda i: (0, 0), pipeline_mode=pl.Buffered(1)),
        pl.BlockSpec((tm, n), lambda i: (i, 0)),
    ]
    args = (a, w, res)
    if norm_g is not None:
        in_specs.append(pl.BlockSpec((1, n), lambda i: (0, 0)))
        args += (norm_g,)
    return pl.pallas_call(
        _out_proj_kernel if norm_g is None else _out_proj_norm_kernel,
        grid=(m // tm,),
        in_specs=in_specs,
        out_specs=pl.BlockSpec((tm, n), lambda i: (i, 0)),
        out_shape=jax.ShapeDtypeStruct((m, n), F32),
        compiler_params=pltpu.CompilerParams(
            dimension_semantics=("arbitrary",), vmem_limit_bytes=VMEM_LIMIT_BYTES),
        name="out_proj" if norm_g is None else "out_proj_norm",
    )(*args)


def _ret_in_kernel(x_ref, g_ref, pos_ref, invf_ref, w_ref, proj_ref, hn_ref, cos_ref, sin_ref,
                   *, n_q_blocks, n_k_blocks, hk, k_scale):
    j = pl.program_id(1)

    @pl.when(j == 0)
    def _():
        hn_ref[...] = _rms_rows(x_ref[...], g_ref[...]).astype(BF16)
        ang = pos_ref[...].astype(F32) * invf_ref[...]
        cos_ref[...] = jnp.cos(ang)
        sin_ref[...] = jnp.sin(ang)

    acc = _dot(hn_ref[...], w_ref[...])
    half = hk // 2

    @pl.when(j < n_q_blocks + n_k_blocks)
    def _():
        scale = jnp.where(j >= n_q_blocks, k_scale, 1.0).astype(F32)
        cos = cos_ref[...] * scale
        sin = sin_ref[...] * scale
        for h in range(acc.shape[1] // hk):
            t1 = acc[:, h * hk:h * hk + half]
            t2 = acc[:, h * hk + half:(h + 1) * hk]
            proj_ref[:, h * hk:h * hk + half] = (t1 * cos - t2 * sin).astype(BF16)
            proj_ref[:, h * hk + half:(h + 1) * hk] = (t2 * cos + t1 * sin).astype(BF16)

    @pl.when(j >= n_q_blocks + n_k_blocks)
    def _():
        proj_ref[...] = acc.astype(BF16)


def _ret_in_proj(x2, g, pos, inv_freq, w, *, dk_total, tm, tn):
    m, d = x2.shape
    n = w.shape[1]
    hk = dk_total // RET_HEADS
    half = hk // 2
    kern = functools.partial(_ret_in_kernel, n_q_blocks=dk_total // tn, n_k_blocks=dk_total // tn,
                             hk=hk, k_scale=float(hk) ** -0.5)
    return pl.pallas_call(
        kern,
        grid=(m // tm, n // tn),
        in_specs=[
            pl.BlockSpec((tm, d), lambda i, j: (i, 0)),
            pl.BlockSpec((1, d), lambda i, j: (0, 0)),
            pl.BlockSpec((tm, 1), lambda i, j: (i, 0)),
            pl.BlockSpec((1, half), lambda i, j: (0, 0)),
            pl.BlockSpec((d, tn), lambda i, j: (0, j)),
        ],
        out_specs=pl.BlockSpec((tm, tn), lambda i, j: (i, j)),
        out_shape=jax.ShapeDtypeStruct((m, n), BF16),
        scratch_shapes=[
            pltpu.VMEM((tm, d), BF16),
            pltpu.VMEM((tm, half), F32),
            pltpu.VMEM((tm, half), F32),
        ],
        compiler_params=pltpu.CompilerParams(
            dimension_semantics=("arbitrary", "arbitrary"), vmem_limit_bytes=VMEM_LIMIT_BYTES),
        name="ret_in_proj",
    )(x2, g, pos, inv_freq, w)


def _ret_rec_kernel(q_ref, k_ref, v_ref, gate_ref, lg_ref, gng_ref, gnb_ref, o_ref,
                    state_ref, dmat_ref, xi_ref, zeta_ref):
    c = CHUNK
    dk = q_ref.shape[1]
    dv = v_ref.shape[1]

    @pl.when(pl.program_id(2) == 0)
    def _():
        state_ref[...] = jnp.zeros_like(state_ref)
        row = lax.broadcasted_iota(jnp.int32, (c, c), 0)
        col = lax.broadcasted_iota(jnp.int32, (c, c), 1)
        dpos = (row - col).astype(F32)
        dmat_ref[...] = jnp.where(dpos >= 0, jnp.exp(lg_ref[:, :c] * jnp.maximum(dpos, 0.0)), 0.0)
        idx_v = lax.broadcasted_iota(jnp.int32, (c, dv), 0).astype(F32)
        xi_ref[...] = jnp.exp(lg_ref[:, :dv] * (idx_v + 1.0))
        idx_k = lax.broadcasted_iota(jnp.int32, (c, dk), 0).astype(F32)
        zeta_ref[...] = jnp.exp(lg_ref[:, :dk] * (c - 1.0 - idx_k))

    for i in range(q_ref.shape[0] // c):
        rows = pl.ds(i * c, c)
        q = q_ref[rows, :]
        k = k_ref[rows, :]
        v = v_ref[rows, :]
        s = lax.dot_general(q, k, _NT, preferred_element_type=F32) * dmat_ref[...]
        st = state_ref[...]
        o = _dot(s.astype(BF16), v) + _dot(q, st.astype(BF16)) * xi_ref[...]
        kz = (k.astype(F32) * zeta_ref[...]).astype(BF16)
        state_ref[...] = (st * jnp.exp(lg_ref[:, :dv] * float(c))
                          + lax.dot_general(kz, v, _TN, preferred_element_type=F32))

        mu = jnp.mean(o, axis=-1, keepdims=True)
        d = o - mu
        var = jnp.mean(d * d, axis=-1, keepdims=True)
        on = d * lax.rsqrt(var + EPS)
        o_ref[rows, :] = ((on * gng_ref[...] + gnb_ref[...]) * _silu(gate_ref[rows, :].astype(F32))).astype(BF16)


def _ret_recurrence(proj, log_gamma, gn_g, gn_b, *, batch, seq, dk_total, dv_total):
    m = proj.shape[0]
    hk = dk_total // RET_HEADS
    hv = dv_total // RET_HEADS
    t_rows = CHUNK * CHUNKS_PER_STEP
    nt = seq // t_rows
    width = log_gamma.shape[-1]
    k_off = dk_total // hk
    v_off = 2 * dk_total // hv
    g_off = (2 * dk_total + dv_total) // hv
    rows = lambda b, h, t: b * nt + t
    return pl.pallas_call(
        _ret_rec_kernel,
        grid=(batch, RET_HEADS, nt),
        in_specs=[
            pl.BlockSpec((t_rows, hk), lambda b, h, t: (rows(b, h, t), h)),
            pl.BlockSpec((t_rows, hk), lambda b, h, t: (rows(b, h, t), k_off + h)),
            pl.BlockSpec((t_rows, hv), lambda b, h, t: (rows(b, h, t), v_off + h)),
            pl.BlockSpec((t_rows, hv), lambda b, h, t: (rows(b, h, t), g_off + h)),
            pl.BlockSpec((None, 1, width), lambda b, h, t: (h, 0, 0)),
            pl.BlockSpec((1, hv), lambda b, h, t: (0, h)),
            pl.BlockSpec((1, hv), lambda b, h, t: (0, h)),
        ],
        out_specs=pl.BlockSpec((t_rows, hv), lambda b, h, t: (rows(b, h, t), h)),
        out_shape=jax.ShapeDtypeStruct((m, dv_total), BF16),
        scratch_shapes=[
            pltpu.VMEM((hk, hv), F32),
            pltpu.VMEM((CHUNK, CHUNK), F32),
            pltpu.VMEM((CHUNK, hv), F32),
            pltpu.VMEM((CHUNK, hk), F32),
        ],
        compiler_params=pltpu.CompilerParams(
            dimension_semantics=("arbitrary", "arbitrary", "arbitrary"), vmem_limit_bytes=VMEM_LIMIT_BYTES),
        name="ret_recurrence",
    )(proj, proj, proj, proj, log_gamma, gn_g, gn_b)


IN_PROJ_TM = 1024
IN_PROJ_TN = 1024
OUT_PROJ_TM = 512


def _gla_layer(h, batch, seq, norm_g, w_in, w_a1, w_a2, b_a, head_g, w_out, out_norm_g):
    d = h.shape[1]
    dk_total = w_a2.shape[1]
    dv_total = w_out.shape[0]
    rank = w_a1.shape[1]
    wa1 = jnp.zeros((d, GLA_GATE_RANK_PAD), BF16).at[:, :rank].set(w_a1.astype(BF16))
    wa2 = jnp.zeros((GLA_GATE_RANK_PAD, dk_total), BF16).at[:rank, :].set(w_a2.astype(BF16))
    proj, la = _gla_in_proj(h, norm_g[None, :], w_in.astype(BF16), wa1, wa2, b_a[None, :],
                            tm=IN_PROJ_TM, tn=IN_PROJ_TN)
    o = _gla_recurrence(proj, la, head_g[None, :], batch=batch, seq=seq, dk_total=dk_total, dv_total=dv_total)
    return _out_proj(o, w_out.astype(BF16), h, out_norm_g, tm=OUT_PROJ_TM)


def _ret_layer(h, pos, batch, seq, norm_g, w_in, gn_g, gn_b, w_out, out_norm_g):
    dv_total = w_out.shape[0]
    dk_total = (w_in.shape[1] - 2 * dv_total) // 2
    hk = dk_total // RET_HEADS
    half = hk // 2
    inv_freq = (ROPE_BASE ** (-jnp.arange(half, dtype=F32) / half))[None, :]
    log_gamma = jnp.log1p(-jnp.exp2(-5.0 - jnp.arange(RET_HEADS, dtype=F32)))
    width = max(CHUNK, hk, dv_total // RET_HEADS)
    log_gamma = jnp.broadcast_to(log_gamma[:, None, None], (RET_HEADS, 1, width))
    proj = _ret_in_proj(h, norm_g[None, :], pos, inv_freq, w_in.astype(BF16), dk_total=dk_total,
                        tm=IN_PROJ_TM, tn=IN_PROJ_TN)
    o = _ret_recurrence(proj, log_gamma, gn_g[None, :], gn_b[None, :],
                        batch=batch, seq=seq, dk_total=dk_total, dv_total=dv_total)
    return _out_proj(o, w_out.astype(BF16), h, out_norm_g, tm=OUT_PROJ_TM)


def kernel(x, positions, gla_norm, gla_w_in, gla_w_a1, gla_w_a2, gla_b_a, gla_head_g, gla_w_out,
           ret_norm, ret_w_in, ret_gn_g, ret_gn_b, ret_w_out, final_norm):
    batch, seq, d = x.shape
    m = batch * seq
    h = x.reshape(m, d)
    pos = positions.reshape(m, 1)
    depth = gla_norm.shape[0] + ret_norm.shape[0]
    assert depth >= 1
    for i in range(depth):
        j = i // 2
        out_norm_g = final_norm[None, :] if i == depth - 1 else None
        if i % 2 == 0:
            h = _gla_layer(h, batch, seq, gla_norm[j], gla_w_in[j], gla_w_a1[j], gla_w_a2[j],
                           gla_b_a[j], gla_head_g[j], gla_w_out[j], out_norm_g)
        else:
            h = _ret_layer(h, pos, batch, seq, ret_norm[j], ret_w_in[j], ret_gn_g[j], ret_gn_b[j], ret_w_out[j],
                           out_norm_g)
    return h.reshape(batch, seq, d)
```

```python
import functools

import jax
import jax.numpy as jnp
from jax import lax
from jax.experimental import pallas as pl
from jax.experimental.pallas import tpu as pltpu

F32 = jnp.float32
BF16 = jnp.bfloat16

EPS = 1e-6
GLA_HEADS = 4
GLA_GATE_RANK_PAD = 128
GLA_GATE_TEMP = 16.0
RET_HEADS = 8
ROPE_BASE = 10000.0

CHUNK = 256
CHUNKS_PER_STEP = 4
FAST_DECAY_LIMIT = 60.0
VMEM_LIMIT_BYTES = 56 * 1024 * 1024

_NT = (((1,), (1,)), ((), ()))
_TN = (((0,), (0,)), ((), ()))


def _dot(a, b):
    return jnp.dot(a, b, preferred_element_type=F32)


def _rms_rows(x, g):
    ms = jnp.mean(x * x, axis=-1, keepdims=True)
    return x * lax.rsqrt(ms + EPS) * g


def _silu(g):
    hg = 0.5 * g
    return hg + hg * jnp.tanh(hg)


def _split3(a):
    hi = a.astype(BF16)
    r1 = a - hi.astype(F32)
    mid = r1.astype(BF16)
    lo = (r1 - mid.astype(F32)).astype(BF16)
    return hi, mid, lo


def _causal(c):
    row = lax.broadcasted_iota(jnp.int32, (c, c), 0)
    col = lax.broadcasted_iota(jnp.int32, (c, c), 1)
    return row >= col


def _gla_in_kernel(x_ref, g_ref, w_ref, wa1_ref, wa2_ref, ba_ref, proj_ref, la_ref, hn_ref):
    @pl.when(pl.program_id(1) == 0)
    def _():
        hn = _rms_rows(x_ref[...], g_ref[...]).astype(BF16)
        hn_ref[...] = hn
        z = _dot(hn, wa1_ref[...])
        xa = _dot(z.astype(BF16), wa2_ref[...]) + ba_ref[...]
        la_ref[...] = (jnp.minimum(xa, 0.0) - jnp.log1p(jnp.exp(-jnp.abs(xa)))) * (1.0 / GLA_GATE_TEMP)

    proj_ref[...] = _dot(hn_ref[...], w_ref[...]).astype(BF16)


def _gla_in_proj(x2, g, w, wa1, wa2, ba, *, tm, tn):
    m, d = x2.shape
    n = w.shape[1]
    dk = wa2.shape[1]
    return pl.pallas_call(
        _gla_in_kernel,
        grid=(m // tm, n // tn),
        in_specs=[
            pl.BlockSpec((tm, d), lambda i, j: (i, 0)),
            pl.BlockSpec((1, d), lambda i, j: (0, 0)),
            pl.BlockSpec((d, tn), lambda i, j: (0, j)),
            pl.BlockSpec((d, GLA_GATE_RANK_PAD), lambda i, j: (0, 0)),
            pl.BlockSpec((GLA_GATE_RANK_PAD, dk), lambda i, j: (0, 0)),
            pl.BlockSpec((1, dk), lambda i, j: (0, 0)),
        ],
        out_specs=[
            pl.BlockSpec((tm, tn), lambda i, j: (i, j)),
            pl.BlockSpec((tm, dk), lambda i, j: (i, 0)),
        ],
        out_shape=[
            jax.ShapeDtypeStruct((m, n), BF16),
            jax.ShapeDtypeStruct((m, dk), F32),
        ],
        scratch_shapes=[pltpu.VMEM((tm, d), BF16)],
        compiler_params=pltpu.CompilerParams(
            dimension_semantics=("arbitrary", "arbitrary"), vmem_limit_bytes=VMEM_LIMIT_BYTES),
        name="gla_in_proj",
    )(x2, g, w, wa1, wa2, ba)


def _gla_rec_kernel(q_ref, k_ref, v_ref, gate_ref, la_ref, hg_ref, o_ref,
                    state_ref, ltri_ref, cum_ref, kf_ref, sc_ref, *, q_scale):
    c = CHUNK
    dv = v_ref.shape[1]

    @pl.when(pl.program_id(2) == 0)
    def _():
        state_ref[...] = jnp.zeros_like(state_ref)
        ltri_ref[...] = jnp.where(_causal(c), 1.0, 0.0).astype(BF16)

    ones = jnp.ones((c, 128), BF16)
    n_chunks = q_ref.shape[0] // c

    def chunk(i, direct):
        rows = pl.ds(i * c, c)
        ltri = ltri_ref[...]
        a_hi, a_mid, a_lo = _split3(la_ref[rows, :])
        cum = _dot(ltri, a_hi) + _dot(ltri, a_mid) + _dot(ltri, a_lo)
        tot_col = (lax.dot_general(a_hi, ones, _TN, preferred_element_type=F32)
                   + lax.dot_general(a_mid, ones, _TN, preferred_element_type=F32)
                   + lax.dot_general(a_lo, ones, _TN, preferred_element_type=F32))
        total = cum[c - 1:c, :]

        qf = q_ref[rows, :].astype(F32) * q_scale
        kf = k_ref[rows, :].astype(F32)
        v = v_ref[rows, :]
        qs = (qf * jnp.exp(cum)).astype(BF16)
        ks = (kf * jnp.exp(total - cum)).astype(BF16)

        if direct:
            kn = (kf * jnp.exp(-cum)).astype(BF16)
            s = lax.dot_general(qs, kn, _NT, preferred_element_type=F32)
        else:
            cum_ref[...] = cum
            kf_ref[...] = kf
            sc_ref[...] = jnp.zeros_like(sc_ref)
            lane = lax.broadcasted_iota(jnp.int32, (1, c), 1)

            def column(s, carry):
                d = jnp.exp(jnp.minimum(cum_ref[...] - cum_ref[pl.ds(s, 1), :], 0.0))
                colv = jnp.sum(qf * kf_ref[pl.ds(s, 1), :] * d, axis=1, keepdims=True)
                sc_ref[...] += colv * jnp.where(lane == s, 1.0, 0.0)
                return carry

            lax.fori_loop(0, c, column, 0)
            s = sc_ref[...]
        p = jnp.where(_causal(c), s, 0.0).astype(BF16)

        st = state_ref[...]
        o = _dot(qs, st.astype(BF16)) + _dot(p, v)
        decay = jnp.exp(tot_col)
        decay = jnp.concatenate([decay] * (dv // 128), axis=1)
        state_ref[...] = st * decay + lax.dot_general(ks, v, _TN, preferred_element_type=F32)

        ms = jnp.mean(o * o, axis=-1, keepdims=True)
        on = o * lax.rsqrt(ms + EPS)
        o_ref[rows, :] = (on * hg_ref[...] * _silu(gate_ref[rows, :].astype(F32))).astype(BF16)

    least = jnp.min(jnp.sum(la_ref[pl.ds(0, c), :], axis=0, keepdims=True))
    for i in range(1, n_chunks):
        least = jnp.minimum(least, jnp.min(jnp.sum(la_ref[pl.ds(i * c, c), :], axis=0, keepdims=True)))
    direct_ok = least >= -FAST_DECAY_LIMIT

    @pl.when(direct_ok)
    def _():
        for i in range(n_chunks):
            chunk(i, True)

    @pl.when(jnp.logical_not(direct_ok))
    def _():
        for i in range(n_chunks):
            chunk(i, False)


def _gla_recurrence(proj, la, head_g, *, batch, seq, dk_total, dv_total):
    m = proj.shape[0]
    hk = dk_total // GLA_HEADS
    hv = dv_total // GLA_HEADS
    t_rows = CHUNK * CHUNKS_PER_STEP
    nt = seq // t_rows
    k_off = dk_total // hk
    v_off = 2 * dk_total // hv
    g_off = (2 * dk_total + dv_total) // hv
    rows = lambda b, h, t: b * nt + t
    return pl.pallas_call(
        functools.partial(_gla_rec_kernel, q_scale=float(hk) ** -0.5),
        grid=(batch, GLA_HEADS, nt),
        in_specs=[
            pl.BlockSpec((t_rows, hk), lambda b, h, t: (rows(b, h, t), h)),
            pl.BlockSpec((t_rows, hk), lambda b, h, t: (rows(b, h, t), k_off + h)),
            pl.BlockSpec((t_rows, hv), lambda b, h, t: (rows(b, h, t), v_off + h)),
            pl.BlockSpec((t_rows, hv), lambda b, h, t: (rows(b, h, t), g_off + h)),
            pl.BlockSpec((t_rows, hk), lambda b, h, t: (rows(b, h, t), h)),
            pl.BlockSpec((1, hv), lambda b, h, t: (0, h)),
        ],
        out_specs=pl.BlockSpec((t_rows, hv), lambda b, h, t: (rows(b, h, t), h)),
        out_shape=jax.ShapeDtypeStruct((m, dv_total), BF16),
        scratch_shapes=[
            pltpu.VMEM((hk, hv), F32),
            pltpu.VMEM((CHUNK, CHUNK), BF16),
            pltpu.VMEM((CHUNK, hk), F32),
            pltpu.VMEM((CHUNK, hk), F32),
            pltpu.VMEM((CHUNK, CHUNK), F32),
        ],
        compiler_params=pltpu.CompilerParams(
            dimension_semantics=("arbitrary", "arbitrary", "arbitrary"), vmem_limit_bytes=VMEM_LIMIT_BYTES),
        name="gla_recurrence",
    )(proj, proj, proj, proj, la, head_g)


def _out_proj_kernel(a_ref, w_ref, res_ref, o_ref):
    o_ref[...] = res_ref[...] + _dot(a_ref[...], w_ref[...])


def _out_proj_norm_kernel(a_ref, w_ref, res_ref, g_ref, o_ref):
    o_ref[...] = _rms_rows(res_ref[...] + _dot(a_ref[...], w_ref[...]), g_ref[...])


def _out_proj(a, w, res, norm_g=None, *, tm):
    m, k = a.shape
    n = w.shape[1]
    in_specs = [
        pl.BlockSpec((tm, k), lambda i: (i, 0)),
        pl.BlockSpec((k, n), lambda i: (0, 0), pipeline_mode=pl.Buffered(1)),
        pl.BlockSpec((tm, n), lambda i: (i, 0)),
    ]
    args = (a, w, res)
    if norm_g is not None:
        in_specs.append(pl.BlockSpec((1, n), lambda i: (0, 0)))
        args += (norm_g,)
    return pl.pallas_call(
        _out_proj_kernel if norm_g is None else _out_proj_norm_kernel,
        grid=(m // tm,),
        in_specs=in_specs,
        out_specs=pl.BlockSpec((tm, n), lambda i: (i, 0)),
        out_shape=jax.ShapeDtypeStruct((m, n), F32),
        compiler_params=pltpu.CompilerParams(
            dimension_semantics=("arbitrary",), vmem_limit_bytes=VMEM_LIMIT_BYTES),
        name="out_proj" if norm_g is None else "out_proj_norm",
    )(*args)


def _ret_in_kernel(x_ref, g_ref, pos_ref, invf_ref, w_ref, proj_ref, hn_ref, cos_ref, sin_ref,
                   *, n_q_blocks, n_k_blocks, hk, k_scale):
    j = pl.program_id(1)

    @pl.when(j == 0)
    def _():
        hn_ref[...] = _rms_rows(x_ref[...], g_ref[...]).astype(BF16)
        ang = pos_ref[...].astype(F32) * invf_ref[...]
        cos_ref[...] = jnp.cos(ang)
        sin_ref[...] = jnp.sin(ang)

    half = hk // 2

    @pl.when(j < n_q_blocks + n_k_blocks)
    def _():
        scale = jnp.where(j >= n_q_blocks, k_scale, 1.0).astype(F32)
        cos = cos_ref[...] * scale
        sin = sin_ref[...] * scale
        acc = _dot(hn_ref[...], w_ref[...])
        for h in range(acc.shape[1] // hk):
            t1 = acc[:, h * hk:h * hk + half]
            t2 = acc[:, h * hk + half:(h + 1) * hk]
            proj_ref[:, h * hk:h * hk + half] = (t1 * cos - t2 * sin).astype(BF16)
            proj_ref[:, h * hk + half:(h + 1) * hk] = (t2 * cos + t1 * sin).astype(BF16)

    @pl.when(j >= n_q_blocks + n_k_blocks)
    def _():
        proj_ref[...] = _dot(hn_ref[...], w_ref[...]).astype(BF16)


def _ret_in_proj(x2, g, pos, inv_freq, w, *, dk_total, tm, tn):
    m, d = x2.shape
    n = w.shape[1]
    hk = dk_total // RET_HEADS
    half = hk // 2
    kern = functools.partial(_ret_in_kernel, n_q_blocks=dk_total // tn, n_k_blocks=dk_total // tn,
                             hk=hk, k_scale=float(hk) ** -0.5)
    return pl.pallas_call(
        kern,
        grid=(m // tm, n // tn),
        in_specs=[
            pl.BlockSpec((tm, d), lambda i, j: (i, 0)),
            pl.BlockSpec((1, d), lambda i, j: (0, 0)),
            pl.BlockSpec((tm, 1), lambda i, j: (i, 0)),
            pl.BlockSpec((1, half), lambda i, j: (0, 0)),
            pl.BlockSpec((d, tn), lambda i, j: (0, j)),
        ],
        out_specs=pl.BlockSpec((tm, tn), lambda i, j: (i, j)),
        out_shape=jax.ShapeDtypeStruct((m, n), BF16),
        scratch_shapes=[
            pltpu.VMEM((tm, d), BF16),
            pltpu.VMEM((tm, half), F32),
            pltpu.VMEM((tm, half), F32),
        ],
        compiler_params=pltpu.CompilerParams(
            dimension_semantics=("arbitrary", "arbitrary"), vmem_limit_bytes=VMEM_LIMIT_BYTES),
        name="ret_in_proj",
    )(x2, g, pos, inv_freq, w)


def _ret_rec_kernel(q_ref, k_ref, v_ref, gate_ref, lg_ref, gng_ref, gnb_ref, o_ref,
                    state_ref, dmat_ref, xi_ref, zeta_ref):
    c = CHUNK
    dk = q_ref.shape[1]
    dv = v_ref.shape[1]

    @pl.when(pl.program_id(2) == 0)
    def _():
        state_ref[...] = jnp.zeros_like(state_ref)
        row = lax.broadcasted_iota(jnp.int32, (c, c), 0)
        col = lax.broadcasted_iota(jnp.int32, (c, c), 1)
        dpos = (row - col).astype(F32)
        dmat_ref[...] = jnp.where(dpos >= 0, jnp.exp(lg_ref[:, :c] * jnp.maximum(dpos, 0.0)), 0.0)
        idx_v = lax.broadcasted_iota(jnp.int32, (c, dv), 0).astype(F32)
        xi_ref[...] = jnp.exp(lg_ref[:, :dv] * (idx_v + 1.0))
        idx_k = lax.broadcasted_iota(jnp.int32, (c, dk), 0).astype(F32)
        zeta_ref[...] = jnp.exp(lg_ref[:, :dk] * (c - 1.0 - idx_k))

    for i in range(q_ref.shape[0] // c):
        rows = pl.ds(i * c, c)
        q = q_ref[rows, :]
        k = k_ref[rows, :]
        v = v_ref[rows, :]
        s = lax.dot_general(q, k, _NT, preferred_element_type=F32) * dmat_ref[...]
        st = state_ref[...]
        o = _dot(s.astype(BF16), v) + _dot(q, st.astype(BF16)) * xi_ref[...]
        kz = (k.astype(F32) * zeta_ref[...]).astype(BF16)
        state_ref[...] = (st * jnp.exp(lg_ref[:, :dv] * float(c))
                          + lax.dot_general(kz, v, _TN, preferred_element_type=F32))

        mu = jnp.mean(o, axis=-1, keepdims=True)
        d = o - mu
        var = jnp.mean(d * d, axis=-1, keepdims=True)
        on = d * lax.rsqrt(var + EPS)
        o_ref[rows, :] = ((on * gng_ref[...] + gnb_ref[...]) * _silu(gate_ref[rows, :].astype(F32))).astype(BF16)


def _ret_recurrence(proj, log_gamma, gn_g, gn_b, *, batch, seq, dk_total, dv_total):
    m = proj.shape[0]
    hk = dk_total // RET_HEADS
    hv = dv_total // RET_HEADS
    t_rows = CHUNK * CHUNKS_PER_STEP
    nt = seq // t_rows
    width = log_gamma.shape[-1]
    k_off = dk_total // hk
    v_off = 2 * dk_total // hv
    g_off = (2 * dk_total + dv_total) // hv
    rows = lambda b, h, t: b * nt + t
    return pl.pallas_call(
        _ret_rec_kernel,
        grid=(batch, RET_HEADS, nt),
        in_specs=[
            pl.BlockSpec((t_rows, hk), lambda b, h, t: (rows(b, h, t), h)),
            pl.BlockSpec((t_rows, hk), lambda b, h, t: (rows(b, h, t), k_off + h)),
            pl.BlockSpec((t_rows, hv), lambda b, h, t: (rows(b, h, t), v_off + h)),
            pl.BlockSpec((t_rows, hv), lambda b, h, t: (rows(b, h, t), g_off + h)),
            pl.BlockSpec((None, 1, width), lambda b, h, t: (h, 0, 0)),
            pl.BlockSpec((1, hv), lambda b, h, t: (0, h)),
            pl.BlockSpec((1, hv), lambda b, h, t: (0, h)),
        ],
        out_specs=pl.BlockSpec((t_rows, hv), lambda b, h, t: (rows(b, h, t), h)),
        out_shape=jax.ShapeDtypeStruct((m, dv_total), BF16),
        scratch_shapes=[
            pltpu.VMEM((hk, hv), F32),
            pltpu.VMEM((CHUNK, CHUNK), F32),
            pltpu.VMEM((CHUNK, hv), F32),
            pltpu.VMEM((CHUNK, hk), F32),
        ],
        compiler_params=pltpu.CompilerParams(
            dimension_semantics=("arbitrary", "arbitrary", "arbitrary"), vmem_limit_bytes=VMEM_LIMIT_BYTES),
        name="ret_recurrence",
    )(proj, proj, proj, proj, log_gamma, gn_g, gn_b)


IN_PROJ_TM = 1024
IN_PROJ_TN = 1024
OUT_PROJ_TM = 512


def _gla_layer(h, batch, seq, norm_g, w_in, w_a1, w_a2, b_a, head_g, w_out, out_norm_g):
    d = h.shape[1]
    dk_total = w_a2.shape[1]
    dv_total = w_out.shape[0]
    rank = w_a1.shape[1]
    wa1 = jnp.zeros((d, GLA_GATE_RANK_PAD), BF16).at[:, :rank].set(w_a1.astype(BF16))
    wa2 = jnp.zeros((GLA_GATE_RANK_PAD, dk_total), BF16).at[:rank, :].set(w_a2.astype(BF16))
    proj, la = _gla_in_proj(h, norm_g[None, :], w_in.astype(BF16), wa1, wa2, b_a[None, :],
                            tm=IN_PROJ_TM, tn=IN_PROJ_TN)
    o = _gla_recurrence(proj, la, head_g[None, :], batch=batch, seq=seq, dk_total=dk_total, dv_total=dv_total)
    return _out_proj(o, w_out.astype(BF16), h, out_norm_g, tm=OUT_PROJ_TM)


def _ret_layer(h, pos, batch, seq, norm_g, w_in, gn_g, gn_b, w_out, out_norm_g):
    dv_total = w_out.shape[0]
    dk_total = (w_in.shape[1] - 2 * dv_total) // 2
    hk = dk_total // RET_HEADS
    half = hk // 2
    inv_freq = (ROPE_BASE ** (-jnp.arange(half, dtype=F32) / half))[None, :]
    log_gamma = jnp.log1p(-jnp.exp2(-5.0 - jnp.arange(RET_HEADS, dtype=F32)))
    width = max(CHUNK, hk, dv_total // RET_HEADS)
    log_gamma = jnp.broadcast_to(log_gamma[:, None, None], (RET_HEADS, 1, width))
    proj = _ret_in_proj(h, norm_g[None, :], pos, inv_freq, w_in.astype(BF16), dk_total=dk_total,
                        tm=IN_PROJ_TM, tn=IN_PROJ_TN)
    o = _ret_recurrence(proj, log_gamma, gn_g[None, :], gn_b[None, :],
                        batch=batch, seq=seq, dk_total=dk_total, dv_total=dv_total)
    return _out_proj(o, w_out.astype(BF16), h, out_norm_g, tm=OUT_PROJ_TM)


def kernel(x, positions, gla_norm, gla_w_in, gla_w_a1, gla_w_a2, gla_b_a, gla_head_g, gla_w_out,
           ret_norm, ret_w_in, ret_gn_g, ret_gn_b, ret_w_out, final_norm):
    batch, seq, d = x.shape
    m = batch * seq
    h = x.reshape(m, d)
    pos = positions.reshape(m, 1)
    depth = gla_norm.shape[0] + ret_norm.shape[0]
    assert depth >= 1
    for i in range(depth):
        j = i // 2
        out_norm_g = final_norm[None, :] if i == depth - 1 else None
        if i % 2 == 0:
            h = _gla_layer(h, batch, seq, gla_norm[j], gla_w_in[j], gla_w_a1[j], gla_w_a2[j],
                           gla_b_a[j], gla_head_g[j], gla_w_out[j], out_norm_g)
        else:
            h = _ret_layer(h, pos, batch, seq, ret_norm[j], ret_w_in[j], ret_gn_g[j], ret_gn_b[j], ret_w_out[j],
                           out_norm_g)
    return h.reshape(batch, seq, d)
```

```python
import functools
import math

import jax
import jax.numpy as jnp
from jax import lax
from jax.experimental import pallas as pl
from jax.experimental.pallas import tpu as pltpu

F32 = jnp.float32
BF16 = jnp.bfloat16

EPS = 1e-6
GLA_HEADS = 4
GLA_GATE_RANK_PAD = 128
GLA_GATE_TEMP = 16.0
RET_HEADS = 8
ROPE_BASE = 10000.0

CHUNK = 256
CHUNKS_PER_STEP = 4
FAST_DECAY_LIMIT = 60.0
VMEM_LIMIT_BYTES = 56 * 1024 * 1024

_NT = (((1,), (1,)), ((), ()))
_TN = (((0,), (0,)), ((), ()))


def _dot(a, b):
    return jnp.dot(a, b, preferred_element_type=F32)


def _rms_rows(x, g):
    ms = jnp.mean(x * x, axis=-1, keepdims=True)
    return x * lax.rsqrt(ms + EPS) * g


def _silu(g):
    hg = 0.5 * g
    return hg + hg * jnp.tanh(hg)


def _split3(a):
    hi = a.astype(BF16)
    r1 = a - hi.astype(F32)
    mid = r1.astype(BF16)
    lo = (r1 - mid.astype(F32)).astype(BF16)
    return hi, mid, lo


def _causal(c):
    row = lax.broadcasted_iota(jnp.int32, (c, c), 0)
    col = lax.broadcasted_iota(jnp.int32, (c, c), 1)
    return row >= col


def _gla_in_kernel(x_ref, g_ref, w_ref, wa1_ref, wa2_ref, ba_ref, proj_ref, la_ref, hn_ref):
    @pl.when(pl.program_id(1) == 0)
    def _():
        hn = _rms_rows(x_ref[...], g_ref[...]).astype(BF16)
        hn_ref[...] = hn
        z = _dot(hn, wa1_ref[...])
        xa = _dot(z.astype(BF16), wa2_ref[...]) + ba_ref[...]
        la_ref[...] = (jnp.minimum(xa, 0.0) - jnp.log1p(jnp.exp(-jnp.abs(xa)))) * (1.0 / GLA_GATE_TEMP)

    proj_ref[...] = _dot(hn_ref[...], w_ref[...]).astype(BF16)


def _gla_in_proj(x2, g, w, wa1, wa2, ba, *, tm, tn):
    m, d = x2.shape
    n = w.shape[1]
    dk = wa2.shape[1]
    return pl.pallas_call(
        _gla_in_kernel,
        grid=(m // tm, n // tn),
        in_specs=[
            pl.BlockSpec((tm, d), lambda i, j: (i, 0)),
            pl.BlockSpec((1, d), lambda i, j: (0, 0)),
            pl.BlockSpec((d, tn), lambda i, j: (0, j)),
            pl.BlockSpec((d, GLA_GATE_RANK_PAD), lambda i, j: (0, 0)),
            pl.BlockSpec((GLA_GATE_RANK_PAD, dk), lambda i, j: (0, 0)),
            pl.BlockSpec((1, dk), lambda i, j: (0, 0)),
        ],
        out_specs=[
            pl.BlockSpec((tm, tn), lambda i, j: (i, j)),
            pl.BlockSpec((tm, dk), lambda i, j: (i, 0)),
        ],
        out_shape=[
            jax.ShapeDtypeStruct((m, n), BF16),
            jax.ShapeDtypeStruct((m, dk), F32),
        ],
        scratch_shapes=[pltpu.VMEM((tm, d), BF16)],
        compiler_params=pltpu.CompilerParams(
            dimension_semantics=("arbitrary", "arbitrary"), vmem_limit_bytes=VMEM_LIMIT_BYTES),
        name="gla_in_proj",
    )(x2, g, w, wa1, wa2, ba)


def _gla_rec_kernel(q_ref, k_ref, v_ref, la_ref, hg_ref, o_ref,
                    state_ref, ltri_ref, cum_ref, kf_ref, sc_ref, *, q_scale):
    c = CHUNK
    dv = v_ref.shape[1]

    @pl.when(pl.program_id(2) == 0)
    def _():
        state_ref[...] = jnp.zeros_like(state_ref)
        ltri_ref[...] = jnp.where(_causal(c), 1.0, 0.0).astype(BF16)

    ones = jnp.ones((c, 128), BF16)
    n_chunks = q_ref.shape[0] // c

    def chunk(i, direct):
        rows = pl.ds(i * c, c)
        ltri = ltri_ref[...]
        a_hi, a_mid, a_lo = _split3(la_ref[rows, :])
        cum = _dot(ltri, a_hi) + _dot(ltri, a_mid) + _dot(ltri, a_lo)
        tot_col = (lax.dot_general(a_hi, ones, _TN, preferred_element_type=F32)
                   + lax.dot_general(a_mid, ones, _TN, preferred_element_type=F32)
                   + lax.dot_general(a_lo, ones, _TN, preferred_element_type=F32))
        total = cum[c - 1:c, :]

        qf = q_ref[rows, :].astype(F32) * q_scale
        kf = k_ref[rows, :].astype(F32)
        v = v_ref[rows, :]
        qs = (qf * jnp.exp(cum)).astype(BF16)
        ks = (kf * jnp.exp(total - cum)).astype(BF16)

        if direct:
            kn = (kf * jnp.exp(-cum)).astype(BF16)
            s = lax.dot_general(qs, kn, _NT, preferred_element_type=F32)
        else:
            cum_ref[...] = cum
            kf_ref[...] = kf
            sc_ref[...] = jnp.zeros_like(sc_ref)
            lane = lax.broadcasted_iota(jnp.int32, (1, c), 1)

            def column(s, carry):
                d = jnp.exp(jnp.minimum(cum_ref[...] - cum_ref[pl.ds(s, 1), :], 0.0))
                colv = jnp.sum(qf * kf_ref[pl.ds(s, 1), :] * d, axis=1, keepdims=True)
                sc_ref[...] += colv * jnp.where(lane == s, 1.0, 0.0)
                return carry

            lax.fori_loop(0, c, column, 0)
            s = sc_ref[...]
        p = jnp.where(_causal(c), s, 0.0).astype(BF16)

        st = state_ref[...]
        o = _dot(qs, st.astype(BF16)) + _dot(p, v)
        decay = jnp.exp(tot_col)
        decay = jnp.concatenate([decay] * (dv // 128), axis=1)
        state_ref[...] = st * decay + lax.dot_general(ks, v, _TN, preferred_element_type=F32)

        ms = jnp.mean(o * o, axis=-1, keepdims=True)
        on = o * lax.rsqrt(ms + EPS)
        o_ref[rows, :] = (on * hg_ref[...]).astype(BF16)

    least = jnp.min(jnp.sum(la_ref[pl.ds(0, c), :], axis=0, keepdims=True))
    for i in range(1, n_chunks):
        least = jnp.minimum(least, jnp.min(jnp.sum(la_ref[pl.ds(i * c, c), :], axis=0, keepdims=True)))
    direct_ok = least >= -FAST_DECAY_LIMIT

    @pl.when(direct_ok)
    def _():
        for i in range(n_chunks):
            chunk(i, True)

    @pl.when(jnp.logical_not(direct_ok))
    def _():
        for i in range(n_chunks):
            chunk(i, False)


def _gla_recurrence(proj, la, head_g, *, batch, seq, dk_total, dv_total):
    m = proj.shape[0]
    hk = dk_total // GLA_HEADS
    hv = dv_total // GLA_HEADS
    t_rows = CHUNK * CHUNKS_PER_STEP
    nt = seq // t_rows
    k_off = dk_total // hk
    v_off = 2 * dk_total // hv
    rows = lambda b, h, t: b * nt + t
    return pl.pallas_call(
        functools.partial(_gla_rec_kernel, q_scale=float(hk) ** -0.5),
        grid=(batch, GLA_HEADS, nt),
        in_specs=[
            pl.BlockSpec((t_rows, hk), lambda b, h, t: (rows(b, h, t), h)),
            pl.BlockSpec((t_rows, hk), lambda b, h, t: (rows(b, h, t), k_off + h)),
            pl.BlockSpec((t_rows, hv), lambda b, h, t: (rows(b, h, t), v_off + h)),
            pl.BlockSpec((t_rows, hk), lambda b, h, t: (rows(b, h, t), h)),
            pl.BlockSpec((1, hv), lambda b, h, t: (0, h)),
        ],
        out_specs=pl.BlockSpec((t_rows, hv), lambda b, h, t: (rows(b, h, t), h)),
        out_shape=jax.ShapeDtypeStruct((m, dv_total), BF16),
        scratch_shapes=[
            pltpu.VMEM((hk, hv), F32),
            pltpu.VMEM((CHUNK, CHUNK), BF16),
            pltpu.VMEM((CHUNK, hk), F32),
            pltpu.VMEM((CHUNK, hk), F32),
            pltpu.VMEM((CHUNK, CHUNK), F32),
        ],
        compiler_params=pltpu.CompilerParams(
            dimension_semantics=("arbitrary", "arbitrary", "arbitrary"), vmem_limit_bytes=VMEM_LIMIT_BYTES),
        name="gla_recurrence",
    )(proj, proj, proj, la, head_g)


def _gated_out(a_ref, gate_refs, w_ref, res_ref):
    gw = gate_refs[0].shape[1]
    y = res_ref[...]
    for b, g_ref in enumerate(gate_refs):
        cols = slice(b * gw, (b + 1) * gw)
        gated = (a_ref[:, cols].astype(F32) * _silu(g_ref[...].astype(F32))).astype(BF16)
        y = y + _dot(gated, w_ref[cols, :])
    return y


def _out_proj_kernel(a_ref, *refs):
    *gate_refs, w_ref, res_ref, o_ref = refs
    o_ref[...] = _gated_out(a_ref, gate_refs, w_ref, res_ref)


def _out_proj_norm_kernel(a_ref, *refs):
    *gate_refs, w_ref, res_ref, g_ref, o_ref = refs
    o_ref[...] = _rms_rows(_gated_out(a_ref, gate_refs, w_ref, res_ref), g_ref[...])


def _out_proj(a, proj, gate_col, w, res, norm_g=None, *, tm):
    m, k = a.shape
    n = w.shape[1]
    gw = math.gcd(gate_col, k)
    gate_specs = [pl.BlockSpec((tm, gw), functools.partial(lambda i, c: (i, c), c=gate_col // gw + b))
                  for b in range(k // gw)]
    in_specs = [pl.BlockSpec((tm, k), lambda i: (i, 0))] + gate_specs + [
        pl.BlockSpec((k, n), lambda i: (0, 0), pipeline_mode=pl.Buffered(1)),
        pl.BlockSpec((tm, n), lambda i: (i, 0)),
    ]
    args = (a,) + (proj,) * len(gate_specs) + (w, res)
    if norm_g is not None:
        in_specs.append(pl.BlockSpec((1, n), lambda i: (0, 0)))
        args += (norm_g,)
    return pl.pallas_call(
        _out_proj_kernel if norm_g is None else _out_proj_norm_kernel,
        grid=(m // tm,),
        in_specs=in_specs,
        out_specs=pl.BlockSpec((tm, n), lambda i: (i, 0)),
        out_shape=jax.ShapeDtypeStruct((m, n), F32),
        compiler_params=pltpu.CompilerParams(
            dimension_semantics=("arbitrary",), vmem_limit_bytes=VMEM_LIMIT_BYTES),
        name="out_proj" if norm_g is None else "out_proj_norm",
    )(*args)


def _ret_in_kernel(x_ref, g_ref, pos_ref, invf_ref, w_ref, proj_ref, hn_ref, cos_ref, sin_ref,
                   *, n_q_blocks, n_k_blocks, hk, k_scale):
    j = pl.program_id(1)

    @pl.when(j == 0)
    def _():
        hn_ref[...] = _rms_rows(x_ref[...], g_ref[...]).astype(BF16)
        ang = pos_ref[...].astype(F32) * invf_ref[...]
        cos_ref[...] = jnp.cos(ang)
        sin_ref[...] = jnp.sin(ang)

    half = hk // 2

    @pl.when(j < n_q_blocks + n_k_blocks)
    def _():
        scale = jnp.where(j >= n_q_blocks, k_scale, 1.0).astype(F32)
        cos = cos_ref[...] * scale
        sin = sin_ref[...] * scale
        acc = _dot(hn_ref[...], w_ref[...])
        for h in range(acc.shape[1] // hk):
            t1 = acc[:, h * hk:h * hk + half]
            t2 = acc[:, h * hk + half:(h + 1) * hk]
            proj_ref[:, h * hk:h * hk + half] = (t1 * cos - t2 * sin).astype(BF16)
            proj_ref[:, h * hk + half:(h + 1) * hk] = (t2 * cos + t1 * sin).astype(BF16)

    @pl.when(j >= n_q_blocks + n_k_blocks)
    def _():
        proj_ref[...] = _dot(hn_ref[...], w_ref[...]).astype(BF16)


def _ret_in_proj(x2, g, pos, inv_freq, w, *, dk_total, tm, tn):
    m, d = x2.shape
    n = w.shape[1]
    hk = dk_total // RET_HEADS
    half = hk // 2
    kern = functools.partial(_ret_in_kernel, n_q_blocks=dk_total // tn, n_k_blocks=dk_total // tn,
                             hk=hk, k_scale=float(hk) ** -0.5)
    return pl.pallas_call(
        kern,
        grid=(m // tm, n // tn),
        in_specs=[
            pl.BlockSpec((tm, d), lambda i, j: (i, 0)),
            pl.BlockSpec((1, d), lambda i, j: (0, 0)),
            pl.BlockSpec((tm, 1), lambda i, j: (i, 0)),
            pl.BlockSpec((1, half), lambda i, j: (0, 0)),
            pl.BlockSpec((d, tn), lambda i, j: (0, j)),
        ],
        out_specs=pl.BlockSpec((tm, tn), lambda i, j: (i, j)),
        out_shape=jax.ShapeDtypeStruct((m, n), BF16),
        scratch_shapes=[
            pltpu.VMEM((tm, d), BF16),
            pltpu.VMEM((tm, half), F32),
            pltpu.VMEM((tm, half), F32),
        ],
        compiler_params=pltpu.CompilerParams(
            dimension_semantics=("arbitrary", "arbitrary"), vmem_limit_bytes=VMEM_LIMIT_BYTES),
        name="ret_in_proj",
    )(x2, g, pos, inv_freq, w)


def _ret_rec_kernel(q_ref, k_ref, v_ref, lg_ref, gng_ref, gnb_ref, o_ref,
                    state_ref, dmat_ref, xi_ref, zeta_ref):
    c = CHUNK
    dk = q_ref.shape[1]
    dv = v_ref.shape[1]

    @pl.when(pl.program_id(2) == 0)
    def _():
        state_ref[...] = jnp.zeros_like(state_ref)
        row = lax.broadcasted_iota(jnp.int32, (c, c), 0)
        col = lax.broadcasted_iota(jnp.int32, (c, c), 1)
        dpos = (row - col).astype(F32)
        dmat_ref[...] = jnp.where(dpos >= 0, jnp.exp(lg_ref[:, :c] * jnp.maximum(dpos, 0.0)), 0.0)
        idx_v = lax.broadcasted_iota(jnp.int32, (c, dv), 0).astype(F32)
        xi_ref[...] = jnp.exp(lg_ref[:, :dv] * (idx_v + 1.0))
        idx_k = lax.broadcasted_iota(jnp.int32, (c, dk), 0).astype(F32)
        zeta_ref[...] = jnp.exp(lg_ref[:, :dk] * (c - 1.0 - idx_k))

    for i in range(q_ref.shape[0] // c):
        rows = pl.ds(i * c, c)
        q = q_ref[rows, :]
        k = k_ref[rows, :]
        v = v_ref[rows, :]
        s = lax.dot_general(q, k, _NT, preferred_element_type=F32) * dmat_ref[...]
        st = state_ref[...]
        o = _dot(s.astype(BF16), v) + _dot(q, st.astype(BF16)) * xi_ref[...]
        kz = (k.astype(F32) * zeta_ref[...]).astype(BF16)
        state_ref[...] = (st * jnp.exp(lg_ref[:, :dv] * float(c))
                          + lax.dot_general(kz, v, _TN, preferred_element_type=F32))

        mu = jnp.mean(o, axis=-1, keepdims=True)
        d = o - mu
        var = jnp.mean(d * d, axis=-1, keepdims=True)
        on = d * lax.rsqrt(var + EPS)
        o_ref[rows, :] = (on * gng_ref[...] + gnb_ref[...]).astype(BF16)


def _ret_recurrence(proj, log_gamma, gn_g, gn_b, *, batch, seq, dk_total, dv_total):
    m = proj.shape[0]
    hk = dk_total // RET_HEADS
    hv = dv_total // RET_HEADS
    t_rows = CHUNK * CHUNKS_PER_STEP
    nt = seq // t_rows
    width = log_gamma.shape[-1]
    k_off = dk_total // hk
    v_off = 2 * dk_total // hv
    rows = lambda b, h, t: b * nt + t
    return pl.pallas_call(
        _ret_rec_kernel,
        grid=(batch, RET_HEADS, nt),
        in_specs=[
            pl.BlockSpec((t_rows, hk), lambda b, h, t: (rows(b, h, t), h)),
            pl.BlockSpec((t_rows, hk), lambda b, h, t: (rows(b, h, t), k_off + h)),
            pl.BlockSpec((t_rows, hv), lambda b, h, t: (rows(b, h, t), v_off + h)),
            pl.BlockSpec((None, 1, width), lambda b, h, t: (h, 0, 0)),
            pl.BlockSpec((1, hv), lambda b, h, t: (0, h)),
            pl.BlockSpec((1, hv), lambda b, h, t: (0, h)),
        ],
        out_specs=pl.BlockSpec((t_rows, hv), lambda b, h, t: (rows(b, h, t), h)),
        out_shape=jax.ShapeDtypeStruct((m, dv_total), BF16),
        scratch_shapes=[
            pltpu.VMEM((hk, hv), F32),
            pltpu.VMEM((CHUNK, CHUNK), F32),
            pltpu.VMEM((CHUNK, hv), F32),
            pltpu.VMEM((CHUNK, hk), F32),
        ],
        compiler_params=pltpu.CompilerParams(
            dimension_semantics=("arbitrary", "arbitrary", "arbitrary"), vmem_limit_bytes=VMEM_LIMIT_BYTES),
        name="ret_recurrence",
    )(proj, proj, proj, log_gamma, gn_g, gn_b)


IN_PROJ_TM = 1024
IN_PROJ_TN = 1024
OUT_PROJ_TM = 512


def _gla_layer(h, batch, seq, norm_g, w_in, w_a1, w_a2, b_a, head_g, w_out, out_norm_g):
    d = h.shape[1]
    dk_total = w_a2.shape[1]
    dv_total = w_out.shape[0]
    rank = w_a1.shape[1]
    wa1 = jnp.zeros((d, GLA_GATE_RANK_PAD), BF16).at[:, :rank].set(w_a1.astype(BF16))
    wa2 = jnp.zeros((GLA_GATE_RANK_PAD, dk_total), BF16).at[:rank, :].set(w_a2.astype(BF16))
    proj, la = _gla_in_proj(h, norm_g[None, :], w_in.astype(BF16), wa1, wa2, b_a[None, :],
                            tm=IN_PROJ_TM, tn=IN_PROJ_TN)
    o = _gla_recurrence(proj, la, head_g[None, :], batch=batch, seq=seq, dk_total=dk_total, dv_total=dv_total)
    return _out_proj(o, proj, 2 * dk_total + dv_total, w_out.astype(BF16), h, out_norm_g, tm=OUT_PROJ_TM)


def _ret_layer(h, pos, batch, seq, norm_g, w_in, gn_g, gn_b, w_out, out_norm_g):
    dv_total = w_out.shape[0]
    dk_total = (w_in.shape[1] - 2 * dv_total) // 2
    hk = dk_total // RET_HEADS
    half = hk // 2
    inv_freq = (ROPE_BASE ** (-jnp.arange(half, dtype=F32) / half))[None, :]
    log_gamma = jnp.log1p(-jnp.exp2(-5.0 - jnp.arange(RET_HEADS, dtype=F32)))
    width = max(CHUNK, hk, dv_total // RET_HEADS)
    log_gamma = jnp.broadcast_to(log_gamma[:, None, None], (RET_HEADS, 1, width))
    proj = _ret_in_proj(h, norm_g[None, :], pos, inv_freq, w_in.astype(BF16), dk_total=dk_total,
                        tm=IN_PROJ_TM, tn=IN_PROJ_TN)
    o = _ret_recurrence(proj, log_gamma, gn_g[None, :], gn_b[None, :],
                        batch=batch, seq=seq, dk_total=dk_total, dv_total=dv_total)
    return _out_proj(o, proj, 2 * dk_total + dv_total, w_out.astype(BF16), h, out_norm_g, tm=OUT_PROJ_TM)


def kernel(x, positions, gla_norm, gla_w_in, gla_w_a1, gla_w_a2, gla_b_a, gla_head_g, gla_w_out,
           ret_norm, ret_w_in, ret_gn_g, ret_gn_b, ret_w_out, final_norm):
    batch, seq, d = x.shape
    m = batch * seq
    h = x.reshape(m, d)
    pos = positions.reshape(m, 1)
    depth = gla_norm.shape[0] + ret_norm.shape[0]
    assert depth >= 1
    for i in range(depth):
        j = i // 2
        out_norm_g = final_norm[None, :] if i == depth - 1 else None
        if i % 2 == 0:
            h = _gla_layer(h, batch, seq, gla_norm[j], gla_w_in[j], gla_w_a1[j], gla_w_a2[j],
                           gla_b_a[j], gla_head_g[j], gla_w_out[j], out_norm_g)
        else:
            h = _ret_layer(h, pos, batch, seq, ret_norm[j], ret_w_in[j], ret_gn_g[j], ret_gn_b[j], ret_w_out[j],
                           out_norm_g)
    return h.reshape(batch, seq, d)
```

```python
import functools
import math

import jax
import jax.numpy as jnp
from jax import lax
from jax.experimental import pallas as pl
from jax.experimental.pallas import tpu as pltpu

F32 = jnp.float32
BF16 = jnp.bfloat16

EPS = 1e-6
GLA_HEADS = 4
GLA_GATE_RANK_PAD = 128
GLA_GATE_TEMP = 16.0
RET_HEADS = 8
ROPE_BASE = 10000.0

CHUNK = 256
GLA_CHUNKS_PER_STEP = 4
RET_CHUNKS_PER_STEP = 8
FAST_DECAY_LIMIT = 60.0
VMEM_LIMIT_BYTES = 56 * 1024 * 1024

_NT = (((1,), (1,)), ((), ()))
_TN = (((0,), (0,)), ((), ()))


def _dot(a, b):
    return jnp.dot(a, b, preferred_element_type=F32)


def _rms_rows(x, g):
    ms = jnp.mean(x * x, axis=-1, keepdims=True)
    return x * lax.rsqrt(ms + EPS) * g


def _silu(g):
    hg = 0.5 * g
    return hg + hg * jnp.tanh(hg)


def _split3(a):
    hi = a.astype(BF16)
    r1 = a - hi.astype(F32)
    mid = r1.astype(BF16)
    lo = (r1 - mid.astype(F32)).astype(BF16)
    return hi, mid, lo


def _causal(c):
    row = lax.broadcasted_iota(jnp.int32, (c, c), 0)
    col = lax.broadcasted_iota(jnp.int32, (c, c), 1)
    return row >= col


def _gla_in_kernel(x_ref, g_ref, w_ref, wa1_ref, wa2_ref, ba_ref, proj_ref, la_ref, hn_ref, z_ref):
    j = pl.program_id(1)

    def project():
        proj_ref[...] = _dot(hn_ref[...], w_ref[...]).astype(BF16)

    @pl.when(j == 0)
    def _():
        hn = _rms_rows(x_ref[...], g_ref[...]).astype(BF16)
        hn_ref[...] = hn
        z_ref[...] = _dot(hn, wa1_ref[...]).astype(BF16)
        project()

    @pl.when(j == 1)
    def _():
        xa = _dot(z_ref[...], wa2_ref[...]) + ba_ref[...]
        la_ref[...] = (jnp.minimum(xa, 0.0) - jnp.log(1.0 + jnp.exp(-jnp.abs(xa)))) * (1.0 / GLA_GATE_TEMP)
        project()

    @pl.when(j > 1)
    def _():
        project()


def _gla_in_proj(x2, g, w, wa1, wa2, ba, *, tm, tn):
    m, d = x2.shape
    n = w.shape[1]
    dk = wa2.shape[1]
    assert n // tn >= 2
    return pl.pallas_call(
        _gla_in_kernel,
        grid=(m // tm, n // tn),
        in_specs=[
            pl.BlockSpec((tm, d), lambda i, j: (i, 0)),
            pl.BlockSpec((1, d), lambda i, j: (0, 0)),
            pl.BlockSpec((d, tn), lambda i, j: (0, j)),
            pl.BlockSpec((d, GLA_GATE_RANK_PAD), lambda i, j: (0, 0)),
            pl.BlockSpec((GLA_GATE_RANK_PAD, dk), lambda i, j: (0, 0)),
            pl.BlockSpec((1, dk), lambda i, j: (0, 0)),
        ],
        out_specs=[
            pl.BlockSpec((tm, tn), lambda i, j: (i, j)),
            pl.BlockSpec((tm, dk), lambda i, j: (i, 0)),
        ],
        out_shape=[
            jax.ShapeDtypeStruct((m, n), BF16),
            jax.ShapeDtypeStruct((m, dk), F32),
        ],
        scratch_shapes=[pltpu.VMEM((tm, d), BF16), pltpu.VMEM((tm, GLA_GATE_RANK_PAD), BF16)],
        compiler_params=pltpu.CompilerParams(
            dimension_semantics=("arbitrary", "arbitrary"), vmem_limit_bytes=VMEM_LIMIT_BYTES),
        name="gla_in_proj",
    )(x2, g, w, wa1, wa2, ba)


def _gla_rec_kernel(q_ref, k_ref, v_ref, la_ref, hg_ref, o_ref,
                    state_ref, ltri_ref, cum_ref, kf_ref, sc_ref, *, q_scale):
    c = CHUNK
    dv = v_ref.shape[1]

    @pl.when(pl.program_id(2) == 0)
    def _():
        state_ref[...] = jnp.zeros_like(state_ref)
        ltri_ref[...] = jnp.where(_causal(c), 1.0, 0.0).astype(BF16)

    pick = jnp.where(lax.broadcasted_iota(jnp.int32, (16, 128), 0) == 0, 1.0, 0.0).astype(BF16)
    n_chunks = q_ref.shape[0] // c

    def chunk(i, direct):
        rows = pl.ds(i * c, c)
        ltri = ltri_ref[...]
        a_hi, a_mid, a_lo = _split3(la_ref[rows, :])
        cum = _dot(ltri, a_hi) + _dot(ltri, a_mid) + _dot(ltri, a_lo)
        total = cum[c - 1:c, :]
        t_hi, t_mid, t_lo = _split3(jnp.broadcast_to(total, (16, total.shape[1])))
        tot_col = (lax.dot_general(t_hi, pick, _TN, preferred_element_type=F32)
                   + lax.dot_general(t_mid, pick, _TN, preferred_element_type=F32)
                   + lax.dot_general(t_lo, pick, _TN, preferred_element_type=F32))

        qf = q_ref[rows, :].astype(F32) * q_scale
        kf = k_ref[rows, :].astype(F32)
        v = v_ref[rows, :]
        qs = (qf * jnp.exp(cum)).astype(BF16)
        ks = (kf * jnp.exp(total - cum)).astype(BF16)

        if direct:
            kn = (kf * jnp.exp(-cum)).astype(BF16)
            s = lax.dot_general(qs, kn, _NT, preferred_element_type=F32)
        else:
            cum_ref[...] = cum
            kf_ref[...] = kf
            sc_ref[...] = jnp.zeros_like(sc_ref)
            lane = lax.broadcasted_iota(jnp.int32, (1, c), 1)

            def column(s, carry):
                d = jnp.exp(jnp.minimum(cum_ref[...] - cum_ref[pl.ds(s, 1), :], 0.0))
                colv = jnp.sum(qf * kf_ref[pl.ds(s, 1), :] * d, axis=1, keepdims=True)
                sc_ref[...] += colv * jnp.where(lane == s, 1.0, 0.0)
                return carry

            lax.fori_loop(0, c, column, 0)
            s = sc_ref[...]
        p = jnp.where(_causal(c), s, 0.0).astype(BF16)

        st = state_ref[...]
        o = _dot(qs, st.astype(BF16)) + _dot(p, v)
        decay = jnp.exp(tot_col)
        decay = jnp.concatenate([decay] * (dv // 128), axis=1)
        state_ref[...] = st * decay + lax.dot_general(ks, v, _TN, preferred_element_type=F32)

        ms = jnp.mean(o * o, axis=-1, keepdims=True)
        on = o * lax.rsqrt(ms + EPS)
        o_ref[rows, :] = (on * hg_ref[...]).astype(BF16)

    least = jnp.min(jnp.sum(la_ref[pl.ds(0, c), :], axis=0, keepdims=True))
    for i in range(1, n_chunks):
        least = jnp.minimum(least, jnp.min(jnp.sum(la_ref[pl.ds(i * c, c), :], axis=0, keepdims=True)))
    direct_ok = least >= -FAST_DECAY_LIMIT

    @pl.when(direct_ok)
    def _():
        for i in range(n_chunks):
            chunk(i, True)

    @pl.when(jnp.logical_not(direct_ok))
    def _():
        for i in range(n_chunks):
            chunk(i, False)


def _gla_recurrence(proj, la, head_g, *, batch, seq, dk_total, dv_total):
    m = proj.shape[0]
    hk = dk_total // GLA_HEADS
    hv = dv_total // GLA_HEADS
    t_rows = CHUNK * GLA_CHUNKS_PER_STEP
    nt = seq // t_rows
    k_off = dk_total // hk
    v_off = 2 * dk_total // hv
    rows = lambda b, h, t: b * nt + t
    return pl.pallas_call(
        functools.partial(_gla_rec_kernel, q_scale=float(hk) ** -0.5),
        grid=(batch, GLA_HEADS, nt),
        in_specs=[
            pl.BlockSpec((t_rows, hk), lambda b, h, t: (rows(b, h, t), h)),
            pl.BlockSpec((t_rows, hk), lambda b, h, t: (rows(b, h, t), k_off + h)),
            pl.BlockSpec((t_rows, hv), lambda b, h, t: (rows(b, h, t), v_off + h)),
            pl.BlockSpec((t_rows, hk), lambda b, h, t: (rows(b, h, t), h)),
            pl.BlockSpec((1, hv), lambda b, h, t: (0, h)),
        ],
        out_specs=pl.BlockSpec((t_rows, hv), lambda b, h, t: (rows(b, h, t), h)),
        out_shape=jax.ShapeDtypeStruct((m, dv_total), BF16),
        scratch_shapes=[
            pltpu.VMEM((hk, hv), F32),
            pltpu.VMEM((CHUNK, CHUNK), BF16),
            pltpu.VMEM((CHUNK, hk), F32),
            pltpu.VMEM((CHUNK, hk), F32),
            pltpu.VMEM((CHUNK, CHUNK), F32),
        ],
        compiler_params=pltpu.CompilerParams(
            dimension_semantics=("arbitrary", "arbitrary", "arbitrary"), vmem_limit_bytes=VMEM_LIMIT_BYTES),
        name="gla_recurrence",
    )(proj, proj, proj, la, head_g)


def _gated_out(a_ref, gate_refs, w_ref, res_ref):
    gw = gate_refs[0].shape[1]
    y = res_ref[...]
    for b, g_ref in enumerate(gate_refs):
        cols = slice(b * gw, (b + 1) * gw)
        gated = (a_ref[:, cols].astype(F32) * _silu(g_ref[...].astype(F32))).astype(BF16)
        y = y + _dot(gated, w_ref[cols, :])
    return y


def _out_proj_final_kernel(a_ref, *refs):
    *gate_refs, w_ref, res_ref, g_ref, o_ref = refs
    o_ref[...] = _rms_rows(_gated_out(a_ref, gate_refs, w_ref, res_ref), g_ref[...])


def _out_proj_next_kernel(a_ref, *refs):
    *gate_refs, w_ref, res_ref, g_ref, o_ref, hn_ref = refs
    y = _gated_out(a_ref, gate_refs, w_ref, res_ref)
    o_ref[...] = y
    hn_ref[...] = _rms_rows(y, g_ref[...]).astype(BF16)


def _out_proj(a, proj, gate_col, w, res, norm_g, *, final, tm):
    m, k = a.shape
    n = w.shape[1]
    gw = math.gcd(gate_col, k)
    gate_specs = [pl.BlockSpec((tm, gw), functools.partial(lambda i, c: (i, c), c=gate_col // gw + b))
                  for b in range(k // gw)]
    row_spec = pl.BlockSpec((tm, n), lambda i: (i, 0))
    in_specs = [pl.BlockSpec((tm, k), lambda i: (i, 0))] + gate_specs + [
        pl.BlockSpec((k, n), lambda i: (0, 0), pipeline_mode=pl.Buffered(1)),
        row_spec,
        pl.BlockSpec((1, n), lambda i: (0, 0)),
    ]
    args = (a,) + (proj,) * len(gate_specs) + (w, res, norm_g)
    y_shape = jax.ShapeDtypeStruct((m, n), F32)
    return pl.pallas_call(
        _out_proj_final_kernel if final else _out_proj_next_kernel,
        grid=(m // tm,),
        in_specs=in_specs,
        out_specs=row_spec if final else [row_spec, row_spec],
        out_shape=y_shape if final else [y_shape, jax.ShapeDtypeStruct((m, n), BF16)],
        compiler_params=pltpu.CompilerParams(
            dimension_semantics=("arbitrary",), vmem_limit_bytes=VMEM_LIMIT_BYTES),
        name="out_proj_final" if final else "out_proj_next",
    )(*args)


def _rotary_kernel(pos_ref, invf_ref, cos_ref, sin_ref):
    ang = pos_ref[...].astype(F32) * invf_ref[...]
    cos_ref[...] = jnp.cos(ang)
    sin_ref[...] = jnp.sin(ang)


def _rotary_tables(pos, inv_freq, *, tm):
    m = pos.shape[0]
    half = inv_freq.shape[1]
    spec = pl.BlockSpec((tm, half), lambda i: (i, 0))
    shape = jax.ShapeDtypeStruct((m, half), F32)
    return pl.pallas_call(
        _rotary_kernel,
        grid=(m // tm,),
        in_specs=[pl.BlockSpec((tm, 1), lambda i: (i, 0)), pl.BlockSpec((1, half), lambda i: (0, 0))],
        out_specs=[spec, spec],
        out_shape=[shape, shape],
        compiler_params=pltpu.CompilerParams(dimension_semantics=("arbitrary",)),
        name="rotary_tables",
    )(pos, inv_freq)


def _ret_in_kernel(hn_ref, cos_ref, sin_ref, w_ref, proj_ref, *, n_q_blocks, n_k_blocks, hk, k_scale):
    j = pl.program_id(1)
    half = hk // 2

    @pl.when(j < n_q_blocks + n_k_blocks)
    def _():
        scale = jnp.where(j >= n_q_blocks, k_scale, 1.0).astype(F32)
        cos = cos_ref[...] * scale
        sin = sin_ref[...] * scale
        acc = _dot(hn_ref[...], w_ref[...])
        for h in range(acc.shape[1] // hk):
            t1 = acc[:, h * hk:h * hk + half]
            t2 = acc[:, h * hk + half:(h + 1) * hk]
            proj_ref[:, h * hk:h * hk + half] = (t1 * cos - t2 * sin).astype(BF16)
            proj_ref[:, h * hk + half:(h + 1) * hk] = (t2 * cos + t1 * sin).astype(BF16)

    @pl.when(j >= n_q_blocks + n_k_blocks)
    def _():
        proj_ref[...] = _dot(hn_ref[...], w_ref[...]).astype(BF16)


def _ret_in_proj(hn, cos, sin, w, *, dk_total, tm, tn):
    m, d = hn.shape
    n = w.shape[1]
    hk = dk_total // RET_HEADS
    half = hk // 2
    assert dk_total % tn == 0
    kern = functools.partial(_ret_in_kernel, n_q_blocks=dk_total // tn, n_k_blocks=dk_total // tn,
                             hk=hk, k_scale=float(hk) ** -0.5)
    return pl.pallas_call(
        kern,
        grid=(m // tm, n // tn),
        in_specs=[
            pl.BlockSpec((tm, d), lambda i, j: (i, 0)),
            pl.BlockSpec((tm, half), lambda i, j: (i, 0)),
            pl.BlockSpec((tm, half), lambda i, j: (i, 0)),
            pl.BlockSpec((d, tn), lambda i, j: (0, j)),
        ],
        out_specs=pl.BlockSpec((tm, tn), lambda i, j: (i, j)),
        out_shape=jax.ShapeDtypeStruct((m, n), BF16),
        compiler_params=pltpu.CompilerParams(
            dimension_semantics=("arbitrary", "arbitrary"), vmem_limit_bytes=VMEM_LIMIT_BYTES),
        name="ret_in_proj",
    )(hn, cos, sin, w)


def _ret_rec_kernel(q_ref, k_ref, v_ref, lg_ref, gng_ref, gnb_ref, o_ref,
                    state_ref, dmat_ref, xi_ref, zeta_ref):
    c = CHUNK
    dk = q_ref.shape[1]
    dv = v_ref.shape[1]

    @pl.when(pl.program_id(2) == 0)
    def _():
        state_ref[...] = jnp.zeros_like(state_ref)
        row = lax.broadcasted_iota(jnp.int32, (c, c), 0)
        col = lax.broadcasted_iota(jnp.int32, (c, c), 1)
        dpos = (row - col).astype(F32)
        dmat_ref[...] = jnp.where(dpos >= 0, jnp.exp(lg_ref[:, :c] * jnp.maximum(dpos, 0.0)), 0.0)
        idx_v = lax.broadcasted_iota(jnp.int32, (c, dv), 0).astype(F32)
        xi_ref[...] = jnp.exp(lg_ref[:, :dv] * (idx_v + 1.0))
        idx_k = lax.broadcasted_iota(jnp.int32, (c, dk), 0).astype(F32)
        zeta_ref[...] = jnp.exp(lg_ref[:, :dk] * (c - 1.0 - idx_k))

    for i in range(q_ref.shape[0] // c):
        rows = pl.ds(i * c, c)
        q = q_ref[rows, :]
        k = k_ref[rows, :]
        v = v_ref[rows, :]
        s = lax.dot_general(q, k, _NT, preferred_element_type=F32) * dmat_ref[...]
        st = state_ref[...]
        o = _dot(s.astype(BF16), v) + _dot(q, st.astype(BF16)) * xi_ref[...]
        kz = (k.astype(F32) * zeta_ref[...]).astype(BF16)
        state_ref[...] = (st * jnp.exp(lg_ref[:, :dv] * float(c))
                          + lax.dot_general(kz, v, _TN, preferred_element_type=F32))

        mu = jnp.mean(o, axis=-1, keepdims=True)
        d = o - mu
        var = jnp.mean(d * d, axis=-1, keepdims=True)
        on = d * lax.rsqrt(var + EPS)
        o_ref[rows, :] = (on * gng_ref[...] + gnb_ref[...]).astype(BF16)


def _ret_recurrence(proj, log_gamma, gn_g, gn_b, *, batch, seq, dk_total, dv_total):
    m = proj.shape[0]
    hk = dk_total // RET_HEADS
    hv = dv_total // RET_HEADS
    t_rows = CHUNK * RET_CHUNKS_PER_STEP
    nt = seq // t_rows
    width = log_gamma.shape[-1]
    k_off = dk_total // hk
    v_off = 2 * dk_total // hv
    rows = lambda b, h, t: b * nt + t
    return pl.pallas_call(
        _ret_rec_kernel,
        grid=(batch, RET_HEADS, nt),
        in_specs=[
            pl.BlockSpec((t_rows, hk), lambda b, h, t: (rows(b, h, t), h)),
            pl.BlockSpec((t_rows, hk), lambda b, h, t: (rows(b, h, t), k_off + h)),
            pl.BlockSpec((t_rows, hv), lambda b, h, t: (rows(b, h, t), v_off + h)),
            pl.BlockSpec((None, 1, width), lambda b, h, t: (h, 0, 0)),
            pl.BlockSpec((1, hv), lambda b, h, t: (0, h)),
            pl.BlockSpec((1, hv), lambda b, h, t: (0, h)),
        ],
        out_specs=pl.BlockSpec((t_rows, hv), lambda b, h, t: (rows(b, h, t), h)),
        out_shape=jax.ShapeDtypeStruct((m, dv_total), BF16),
        scratch_shapes=[
            pltpu.VMEM((hk, hv), F32),
            pltpu.VMEM((CHUNK, CHUNK), F32),
            pltpu.VMEM((CHUNK, hv), F32),
            pltpu.VMEM((CHUNK, hk), F32),
        ],
        compiler_params=pltpu.CompilerParams(
            dimension_semantics=("arbitrary", "arbitrary", "arbitrary"), vmem_limit_bytes=VMEM_LIMIT_BYTES),
        name="ret_recurrence",
    )(proj, proj, proj, log_gamma, gn_g, gn_b)


GLA_IN_PROJ_TILE = (1024, 1024)
RET_IN_PROJ_TILE = (1024, 2048)
OUT_PROJ_TM = {True: 512, False: 256}
ROTARY_TM = 2048


def _gla_layer(h, batch, seq, norm_g, w_in, w_a1, w_a2, b_a, head_g, w_out, out_norm_g, final):
    d = h.shape[1]
    dk_total = w_a2.shape[1]
    dv_total = w_out.shape[0]
    rank = w_a1.shape[1]
    wa1 = jnp.zeros((d, GLA_GATE_RANK_PAD), BF16).at[:, :rank].set(w_a1.astype(BF16))
    wa2 = jnp.zeros((GLA_GATE_RANK_PAD, dk_total), BF16).at[:rank, :].set(w_a2.astype(BF16))
    tm, tn = GLA_IN_PROJ_TILE
    proj, la = _gla_in_proj(h, norm_g[None, :], w_in.astype(BF16), wa1, wa2, b_a[None, :], tm=tm, tn=tn)
    o = _gla_recurrence(proj, la, head_g[None, :], batch=batch, seq=seq, dk_total=dk_total, dv_total=dv_total)
    return _out_proj(o, proj, 2 * dk_total + dv_total, w_out.astype(BF16), h, out_norm_g, final=final,
                     tm=OUT_PROJ_TM[final])


def _ret_layer(h, hn, pos, batch, seq, w_in, gn_g, gn_b, w_out, out_norm_g, final):
    dv_total = w_out.shape[0]
    dk_total = (w_in.shape[1] - 2 * dv_total) // 2
    hk = dk_total // RET_HEADS
    half = hk // 2
    inv_freq = (ROPE_BASE ** (-jnp.arange(half, dtype=F32) / half))[None, :]
    log_gamma = jnp.log1p(-jnp.exp2(-5.0 - jnp.arange(RET_HEADS, dtype=F32)))
    width = max(CHUNK, hk, dv_total // RET_HEADS)
    log_gamma = jnp.broadcast_to(log_gamma[:, None, None], (RET_HEADS, 1, width))
    cos, sin = _rotary_tables(pos, inv_freq, tm=ROTARY_TM)
    tm, tn = RET_IN_PROJ_TILE
    proj = _ret_in_proj(hn, cos, sin, w_in.astype(BF16), dk_total=dk_total, tm=tm, tn=tn)
    o = _ret_recurrence(proj, log_gamma, gn_g[None, :], gn_b[None, :],
                        batch=batch, seq=seq, dk_total=dk_total, dv_total=dv_total)
    return _out_proj(o, proj, 2 * dk_total + dv_total, w_out.astype(BF16), h, out_norm_g, final=final,
                     tm=OUT_PROJ_TM[final])


def kernel(x, positions, gla_norm, gla_w_in, gla_w_a1, gla_w_a2, gla_b_a, gla_head_g, gla_w_out,
           ret_norm, ret_w_in, ret_gn_g, ret_gn_b, ret_w_out, final_norm):
    batch, seq, d = x.shape
    m = batch * seq
    h = x.reshape(m, d)
    hn = None
    pos = positions.reshape(m, 1)
    depth = gla_norm.shape[0] + ret_norm.shape[0]
    assert depth >= 1
    for i in range(depth):
        j = i // 2
        final = i == depth - 1
        if final:
            out_norm_g = final_norm
        else:
            out_norm_g = ret_norm[j] if i % 2 == 0 else gla_norm[j + 1]
        if i % 2 == 0:
            out = _gla_layer(h, batch, seq, gla_norm[j], gla_w_in[j], gla_w_a1[j], gla_w_a2[j],
                             gla_b_a[j], gla_head_g[j], gla_w_out[j], out_norm_g[None, :], final)
        else:
            out = _ret_layer(h, hn, pos, batch, seq, ret_w_in[j], ret_gn_g[j], ret_gn_b[j], ret_w_out[j],
                             out_norm_g[None, :], final)
        h, hn = (out, None) if final else out
    return h.reshape(batch, seq, d)
```

```python
import functools
import math

import jax
import jax.numpy as jnp
from jax import lax
from jax.experimental import pallas as pl
from jax.experimental.pallas import tpu as pltpu

F32 = jnp.float32
BF16 = jnp.bfloat16

EPS = 1e-6
GLA_HEADS = 4
GLA_GATE_RANK_PAD = 128
GLA_GATE_TEMP = 16.0
RET_HEADS = 8
ROPE_BASE = 10000.0

CHUNK = 256
HEADS_PER_STEP = 2
GLA_CHUNKS_PER_STEP = 2
RET_CHUNKS_PER_STEP = 4
FAST_DECAY_LIMIT = 60.0
VMEM_LIMIT_BYTES = 56 * 1024 * 1024

_NT = (((1,), (1,)), ((), ()))
_TN = (((0,), (0,)), ((), ()))


def _dot(a, b):
    return jnp.dot(a, b, preferred_element_type=F32)


def _rms_rows(x, g):
    ms = jnp.mean(x * x, axis=-1, keepdims=True)
    return x * lax.rsqrt(ms + EPS) * g


def _silu(g):
    hg = 0.5 * g
    return hg + hg * jnp.tanh(hg)


def _split3(a):
    hi = a.astype(BF16)
    r1 = a - hi.astype(F32)
    mid = r1.astype(BF16)
    lo = (r1 - mid.astype(F32)).astype(BF16)
    return hi, mid, lo


def _causal(c):
    row = lax.broadcasted_iota(jnp.int32, (c, c), 0)
    col = lax.broadcasted_iota(jnp.int32, (c, c), 1)
    return row >= col


def _gla_in_kernel(x_ref, g_ref, w_ref, wa1_ref, wa2_ref, ba_ref, proj_ref, la_ref, hn_ref, z_ref):
    j = pl.program_id(1)

    def project():
        proj_ref[...] = _dot(hn_ref[...], w_ref[...]).astype(BF16)

    @pl.when(j == 0)
    def _():
        hn = _rms_rows(x_ref[...], g_ref[...]).astype(BF16)
        hn_ref[...] = hn
        z_ref[...] = _dot(hn, wa1_ref[...]).astype(BF16)
        project()

    @pl.when(j == 1)
    def _():
        xa = _dot(z_ref[...], wa2_ref[...]) + ba_ref[...]
        la_ref[...] = (jnp.minimum(xa, 0.0) - jnp.log(1.0 + jnp.exp(-jnp.abs(xa)))) * (1.0 / GLA_GATE_TEMP)
        project()

    @pl.when(j > 1)
    def _():
        project()


def _gla_in_proj(x2, g, w, wa1, wa2, ba, *, tm, tn):
    m, d = x2.shape
    n = w.shape[1]
    dk = wa2.shape[1]
    assert n // tn >= 2
    return pl.pallas_call(
        _gla_in_kernel,
        grid=(m // tm, n // tn),
        in_specs=[
            pl.BlockSpec((tm, d), lambda i, j: (i, 0)),
            pl.BlockSpec((1, d), lambda i, j: (0, 0)),
            pl.BlockSpec((d, tn), lambda i, j: (0, j)),
            pl.BlockSpec((d, GLA_GATE_RANK_PAD), lambda i, j: (0, 0)),
            pl.BlockSpec((GLA_GATE_RANK_PAD, dk), lambda i, j: (0, 0)),
            pl.BlockSpec((1, dk), lambda i, j: (0, 0)),
        ],
        out_specs=[
            pl.BlockSpec((tm, tn), lambda i, j: (i, j)),
            pl.BlockSpec((tm, dk), lambda i, j: (i, 0)),
        ],
        out_shape=[
            jax.ShapeDtypeStruct((m, n), BF16),
            jax.ShapeDtypeStruct((m, dk), F32),
        ],
        scratch_shapes=[pltpu.VMEM((tm, d), BF16), pltpu.VMEM((tm, GLA_GATE_RANK_PAD), BF16)],
        compiler_params=pltpu.CompilerParams(
            dimension_semantics=("arbitrary", "arbitrary"), vmem_limit_bytes=VMEM_LIMIT_BYTES),
        name="gla_in_proj",
    )(x2, g, w, wa1, wa2, ba)


def _gla_rec_kernel(q_ref, k_ref, v_ref, la_ref, hg_ref, o_ref,
                    state_ref, ltri_ref, cum_ref, kf_ref, sc_ref, *, q_scale):
    c = CHUNK
    n_heads, dk, dv = state_ref.shape
    n_chunks = q_ref.shape[0] // c

    @pl.when(pl.program_id(2) == 0)
    def _():
        state_ref[...] = jnp.zeros_like(state_ref)
        ltri_ref[...] = jnp.where(_causal(c), 1.0, 0.0).astype(BF16)

    ones = jnp.ones((c, 128), BF16)

    def chunk(i, hd, direct):
        rows = pl.ds(i * c, c)
        kcols = slice(hd * dk, (hd + 1) * dk)
        vcols = slice(hd * dv, (hd + 1) * dv)
        ltri = ltri_ref[...]
        a_hi, a_mid, a_lo = _split3(la_ref[rows, kcols])
        cum = _dot(ltri, a_hi) + _dot(ltri, a_mid) + _dot(ltri, a_lo)
        tot_col = (lax.dot_general(a_hi, ones, _TN, preferred_element_type=F32)
                   + lax.dot_general(a_mid, ones, _TN, preferred_element_type=F32)
                   + lax.dot_general(a_lo, ones, _TN, preferred_element_type=F32))
        total = cum[c - 1:c, :]

        qf = q_ref[rows, kcols].astype(F32) * q_scale
        kf = k_ref[rows, kcols].astype(F32)
        v = v_ref[rows, vcols]
        qs = (qf * jnp.exp(cum)).astype(BF16)
        ks = (kf * jnp.exp(total - cum)).astype(BF16)

        if direct:
            kn = (kf * jnp.exp(-cum)).astype(BF16)
            s = lax.dot_general(qs, kn, _NT, preferred_element_type=F32)
        else:
            cum_ref[...] = cum
            kf_ref[...] = kf
            sc_ref[...] = jnp.zeros_like(sc_ref)
            lane = lax.broadcasted_iota(jnp.int32, (1, c), 1)

            def column(s, carry):
                d = jnp.exp(jnp.minimum(cum_ref[...] - cum_ref[pl.ds(s, 1), :], 0.0))
                colv = jnp.sum(qf * kf_ref[pl.ds(s, 1), :] * d, axis=1, keepdims=True)
                sc_ref[...] += colv * jnp.where(lane == s, 1.0, 0.0)
                return carry

            lax.fori_loop(0, c, column, 0)
            s = sc_ref[...]
        p = jnp.where(_causal(c), s, 0.0).astype(BF16)

        st = state_ref[hd]
        o = _dot(qs, st.astype(BF16)) + _dot(p, v)
        decay = jnp.exp(tot_col)
        decay = jnp.concatenate([decay] * (dv // 128), axis=1)
        state_ref[hd] = st * decay + lax.dot_general(ks, v, _TN, preferred_element_type=F32)

        ms = jnp.mean(o * o, axis=-1, keepdims=True)
        on = o * lax.rsqrt(ms + EPS)
        o_ref[rows, vcols] = (on * hg_ref[:, vcols]).astype(BF16)

    def step(direct):
        for i in range(n_chunks):
            for hd in range(n_heads):
                chunk(i, hd, direct)

    least = jnp.min(jnp.sum(la_ref[pl.ds(0, c), :], axis=0, keepdims=True))
    for i in range(1, n_chunks):
        least = jnp.minimum(least, jnp.min(jnp.sum(la_ref[pl.ds(i * c, c), :], axis=0, keepdims=True)))
    direct_ok = least >= -FAST_DECAY_LIMIT

    @pl.when(direct_ok)
    def _():
        step(True)

    @pl.when(jnp.logical_not(direct_ok))
    def _():
        step(False)


def _gla_recurrence(proj, la, head_g, *, batch, seq, dk_total, dv_total):
    m = proj.shape[0]
    hps = HEADS_PER_STEP
    hk = dk_total // GLA_HEADS
    hv = dv_total // GLA_HEADS
    t_rows = CHUNK * GLA_CHUNKS_PER_STEP
    nt = seq // t_rows
    k_off = dk_total // (hps * hk)
    v_off = 2 * dk_total // (hps * hv)
    rows = lambda b, h, t: b * nt + t
    return pl.pallas_call(
        functools.partial(_gla_rec_kernel, q_scale=float(hk) ** -0.5),
        grid=(batch, GLA_HEADS // hps, nt),
        in_specs=[
            pl.BlockSpec((t_rows, hps * hk), lambda b, h, t: (rows(b, h, t), h)),
            pl.BlockSpec((t_rows, hps * hk), lambda b, h, t: (rows(b, h, t), k_off + h)),
            pl.BlockSpec((t_rows, hps * hv), lambda b, h, t: (rows(b, h, t), v_off + h)),
            pl.BlockSpec((t_rows, hps * hk), lambda b, h, t: (rows(b, h, t), h)),
            pl.BlockSpec((1, hps * hv), lambda b, h, t: (0, h)),
        ],
        out_specs=pl.BlockSpec((t_rows, hps * hv), lambda b, h, t: (rows(b, h, t), h)),
        out_shape=jax.ShapeDtypeStruct((m, dv_total), BF16),
        scratch_shapes=[
            pltpu.VMEM((hps, hk, hv), F32),
            pltpu.VMEM((CHUNK, CHUNK), BF16),
            pltpu.VMEM((CHUNK, hk), F32),
            pltpu.VMEM((CHUNK, hk), F32),
            pltpu.VMEM((CHUNK, CHUNK), F32),
        ],
        compiler_params=pltpu.CompilerParams(
            dimension_semantics=("arbitrary", "arbitrary", "arbitrary"), vmem_limit_bytes=VMEM_LIMIT_BYTES),
        name="gla_recurrence",
    )(proj, proj, proj, la, head_g)


def _gated_out(a_ref, gate_refs, w_ref, res_ref):
    gw = gate_refs[0].shape[1]
    y = res_ref[...]
    for b, g_ref in enumerate(gate_refs):
        cols = slice(b * gw, (b + 1) * gw)
        gated = (a_ref[:, cols].astype(F32) * _silu(g_ref[...].astype(F32))).astype(BF16)
        y = y + _dot(gated, w_ref[cols, :])
    return y


def _out_proj_final_kernel(a_ref, *refs):
    *gate_refs, w_ref, res_ref, g_ref, o_ref = refs
    o_ref[...] = _rms_rows(_gated_out(a_ref, gate_refs, w_ref, res_ref), g_ref[...])


def _out_proj_next_kernel(a_ref, *refs):
    *gate_refs, w_ref, res_ref, g_ref, pos_ref, invf_ref, o_ref, hn_ref, cos_ref, sin_ref = refs
    y = _gated_out(a_ref, gate_refs, w_ref, res_ref)
    o_ref[...] = y
    hn_ref[...] = _rms_rows(y, g_ref[...]).astype(BF16)
    ang = pos_ref[...].astype(F32) * invf_ref[...]
    cos_ref[...] = jnp.cos(ang)
    sin_ref[...] = jnp.sin(ang)


def _out_proj(a, proj, gate_col, w, res, norm_g, pos=None, inv_freq=None, *, final, tm):
    m, k = a.shape
    n = w.shape[1]
    gw = math.gcd(gate_col, k)
    gate_specs = [pl.BlockSpec((tm, gw), functools.partial(lambda i, c: (i, c), c=gate_col // gw + b))
                  for b in range(k // gw)]
    row_spec = pl.BlockSpec((tm, n), lambda i: (i, 0))
    in_specs = [pl.BlockSpec((tm, k), lambda i: (i, 0))] + gate_specs + [
        pl.BlockSpec((k, n), lambda i: (0, 0), pipeline_mode=pl.Buffered(1)),
        row_spec,
        pl.BlockSpec((1, n), lambda i: (0, 0)),
    ]
    args = (a,) + (proj,) * len(gate_specs) + (w, res, norm_g)
    y_shape = jax.ShapeDtypeStruct((m, n), F32)
    if final:
        out_specs, out_shape = row_spec, y_shape
    else:
        half = inv_freq.shape[1]
        in_specs += [pl.BlockSpec((tm, 1), lambda i: (i, 0)), pl.BlockSpec((1, half), lambda i: (0, 0))]
        args += (pos, inv_freq)
        rot_spec = pl.BlockSpec((tm, half), lambda i: (i, 0))
        rot_shape = jax.ShapeDtypeStruct((m, half), F32)
        out_specs = [row_spec, row_spec, rot_spec, rot_spec]
        out_shape = [y_shape, jax.ShapeDtypeStruct((m, n), BF16), rot_shape, rot_shape]
    return pl.pallas_call(
        _out_proj_final_kernel if final else _out_proj_next_kernel,
        grid=(m // tm,),
        in_specs=in_specs,
        out_specs=out_specs,
        out_shape=out_shape,
        compiler_params=pltpu.CompilerParams(
            dimension_semantics=("arbitrary",), vmem_limit_bytes=VMEM_LIMIT_BYTES),
        name="out_proj_final" if final else "out_proj_next",
    )(*args)


def _ret_in_kernel(hn_ref, cos_ref, sin_ref, w_ref, proj_ref, *, n_q_blocks, n_k_blocks, hk, k_scale):
    j = pl.program_id(1)
    half = hk // 2

    @pl.when(j < n_q_blocks + n_k_blocks)
    def _():
        scale = jnp.where(j >= n_q_blocks, k_scale, 1.0).astype(F32)
        cos = cos_ref[...] * scale
        sin = sin_ref[...] * scale
        acc = _dot(hn_ref[...], w_ref[...])
        for h in range(acc.shape[1] // hk):
            t1 = acc[:, h * hk:h * hk + half]
            t2 = acc[:, h * hk + half:(h + 1) * hk]
            proj_ref[:, h * hk:h * hk + half] = (t1 * cos - t2 * sin).astype(BF16)
            proj_ref[:, h * hk + half:(h + 1) * hk] = (t2 * cos + t1 * sin).astype(BF16)

    @pl.when(j >= n_q_blocks + n_k_blocks)
    def _():
        proj_ref[...] = _dot(hn_ref[...], w_ref[...]).astype(BF16)


def _ret_in_proj(hn, cos, sin, w, *, dk_total, tm, tn):
    m, d = hn.shape
    n = w.shape[1]
    hk = dk_total // RET_HEADS
    half = hk // 2
    assert dk_total % tn == 0
    kern = functools.partial(_ret_in_kernel, n_q_blocks=dk_total // tn, n_k_blocks=dk_total // tn,
                             hk=hk, k_scale=float(hk) ** -0.5)
    return pl.pallas_call(
        kern,
        grid=(m // tm, n // tn),
        in_specs=[
            pl.BlockSpec((tm, d), lambda i, j: (i, 0)),
            pl.BlockSpec((tm, half), lambda i, j: (i, 0)),
            pl.BlockSpec((tm, half), lambda i, j: (i, 0)),
            pl.BlockSpec((d, tn), lambda i, j: (0, j)),
        ],
        out_specs=pl.BlockSpec((tm, tn), lambda i, j: (i, j)),
        out_shape=jax.ShapeDtypeStruct((m, n), BF16),
        compiler_params=pltpu.CompilerParams(
            dimension_semantics=("arbitrary", "arbitrary"), vmem_limit_bytes=VMEM_LIMIT_BYTES),
        name="ret_in_proj",
    )(hn, cos, sin, w)


def _ret_rec_kernel(q_ref, k_ref, v_ref, lg_ref, gng_ref, gnb_ref, o_ref,
                    state_ref, dmat_ref, xi_ref, zeta_ref):
    c = CHUNK
    n_heads, dk, dv = state_ref.shape

    @pl.when(pl.program_id(2) == 0)
    def _():
        state_ref[...] = jnp.zeros_like(state_ref)
        row = lax.broadcasted_iota(jnp.int32, (c, c), 0)
        col = lax.broadcasted_iota(jnp.int32, (c, c), 1)
        dpos = (row - col).astype(F32)
        idx_v = lax.broadcasted_iota(jnp.int32, (c, dv), 0).astype(F32)
        idx_k = lax.broadcasted_iota(jnp.int32, (c, dk), 0).astype(F32)
        for hd in range(n_heads):
            dmat_ref[hd] = jnp.where(dpos >= 0, jnp.exp(lg_ref[hd, :, :c] * jnp.maximum(dpos, 0.0)), 0.0)
            xi_ref[hd] = jnp.exp(lg_ref[hd, :, :dv] * (idx_v + 1.0))
            zeta_ref[hd] = jnp.exp(lg_ref[hd, :, :dk] * (c - 1.0 - idx_k))

    def chunk(i, hd):
        rows = pl.ds(i * c, c)
        kcols = slice(hd * dk, (hd + 1) * dk)
        vcols = slice(hd * dv, (hd + 1) * dv)
        q = q_ref[rows, kcols]
        k = k_ref[rows, kcols]
        v = v_ref[rows, vcols]
        s = lax.dot_general(q, k, _NT, preferred_element_type=F32) * dmat_ref[hd]
        st = state_ref[hd]
        o = _dot(s.astype(BF16), v) + _dot(q, st.astype(BF16)) * xi_ref[hd]
        kz = (k.astype(F32) * zeta_ref[hd]).astype(BF16)
        state_ref[hd] = (st * jnp.exp(lg_ref[hd, :, :dv] * float(c))
                         + lax.dot_general(kz, v, _TN, preferred_element_type=F32))

        mu = jnp.mean(o, axis=-1, keepdims=True)
        d = o - mu
        var = jnp.mean(d * d, axis=-1, keepdims=True)
        on = d * lax.rsqrt(var + EPS)
        o_ref[rows, vcols] = (on * gng_ref[:, vcols] + gnb_ref[:, vcols]).astype(BF16)

    for i in range(q_ref.shape[0] // c):
        for hd in range(n_heads):
            chunk(i, hd)


def _ret_recurrence(proj, log_gamma, gn_g, gn_b, *, batch, seq, dk_total, dv_total):
    m = proj.shape[0]
    hps = HEADS_PER_STEP
    hk = dk_total // RET_HEADS
    hv = dv_total // RET_HEADS
    t_rows = CHUNK * RET_CHUNKS_PER_STEP
    nt = seq // t_rows
    width = log_gamma.shape[-1]
    k_off = dk_total // (hps * hk)
    v_off = 2 * dk_total // (hps * hv)
    rows = lambda b, h, t: b * nt + t
    return pl.pallas_call(
        _ret_rec_kernel,
        grid=(batch, RET_HEADS // hps, nt),
        in_specs=[
            pl.BlockSpec((t_rows, hps * hk), lambda b, h, t: (rows(b, h, t), h)),
            pl.BlockSpec((t_rows, hps * hk), lambda b, h, t: (rows(b, h, t), k_off + h)),
            pl.BlockSpec((t_rows, hps * hv), lambda b, h, t: (rows(b, h, t), v_off + h)),
            pl.BlockSpec((hps, 1, width), lambda b, h, t: (h, 0, 0)),
            pl.BlockSpec((1, hps * hv), lambda b, h, t: (0, h)),
            pl.BlockSpec((1, hps * hv), lambda b, h, t: (0, h)),
        ],
        out_specs=pl.BlockSpec((t_rows, hps * hv), lambda b, h, t: (rows(b, h, t), h)),
        out_shape=jax.ShapeDtypeStruct((m, dv_total), BF16),
        scratch_shapes=[
            pltpu.VMEM((hps, hk, hv), F32),
            pltpu.VMEM((hps, CHUNK, CHUNK), F32),
            pltpu.VMEM((hps, CHUNK, hv), F32),
            pltpu.VMEM((hps, CHUNK, hk), F32),
        ],
        compiler_params=pltpu.CompilerParams(
            dimension_semantics=("arbitrary", "arbitrary", "arbitrary"), vmem_limit_bytes=VMEM_LIMIT_BYTES),
        name="ret_recurrence",
    )(proj, proj, proj, log_gamma, gn_g, gn_b)


GLA_IN_PROJ_TILE = (1024, 1024)
RET_IN_PROJ_TILE = (1024, 2048)
OUT_PROJ_TM = {True: 512, False: 256}


def _gla_layer(h, batch, seq, norm_g, w_in, w_a1, w_a2, b_a, head_g, w_out, out_norm_g, pos, inv_freq, final):
    d = h.shape[1]
    dk_total = w_a2.shape[1]
    dv_total = w_out.shape[0]
    rank = w_a1.shape[1]
    wa1 = jnp.zeros((d, GLA_GATE_RANK_PAD), BF16).at[:, :rank].set(w_a1.astype(BF16))
    wa2 = jnp.zeros((GLA_GATE_RANK_PAD, dk_total), BF16).at[:rank, :].set(w_a2.astype(BF16))
    tm, tn = GLA_IN_PROJ_TILE
    proj, la = _gla_in_proj(h, norm_g[None, :], w_in.astype(BF16), wa1, wa2, b_a[None, :], tm=tm, tn=tn)
    o = _gla_recurrence(proj, la, head_g[None, :], batch=batch, seq=seq, dk_total=dk_total, dv_total=dv_total)
    return _out_proj(o, proj, 2 * dk_total + dv_total, w_out.astype(BF16), h, out_norm_g, pos, inv_freq,
                     final=final, tm=OUT_PROJ_TM[final])


def _ret_layer(h, hn, cos, sin, batch, seq, w_in, gn_g, gn_b, w_out, out_norm_g, pos, inv_freq, final):
    dv_total = w_out.shape[0]
    dk_total = (w_in.shape[1] - 2 * dv_total) // 2
    hk = dk_total // RET_HEADS
    log_gamma = jnp.log1p(-jnp.exp2(-5.0 - jnp.arange(RET_HEADS, dtype=F32)))
    width = max(CHUNK, hk, dv_total // RET_HEADS)
    log_gamma = jnp.broadcast_to(log_gamma[:, None, None], (RET_HEADS, 1, width))
    tm, tn = RET_IN_PROJ_TILE
    proj = _ret_in_proj(hn, cos, sin, w_in.astype(BF16), dk_total=dk_total, tm=tm, tn=tn)
    o = _ret_recurrence(proj, log_gamma, gn_g[None, :], gn_b[None, :],
                        batch=batch, seq=seq, dk_total=dk_total, dv_total=dv_total)
    return _out_proj(o, proj, 2 * dk_total + dv_total, w_out.astype(BF16), h, out_norm_g, pos, inv_freq,
                     final=final, tm=OUT_PROJ_TM[final])


def kernel(x, positions, gla_norm, gla_w_in, gla_w_a1, gla_w_a2, gla_b_a, gla_head_g, gla_w_out,
           ret_norm, ret_w_in, ret_gn_g, ret_gn_b, ret_w_out, final_norm):
    batch, seq, d = x.shape
    m = batch * seq
    h = x.reshape(m, d)
    pos = positions.reshape(m, 1)
    ret_half = (ret_w_in.shape[2] - 2 * ret_w_out.shape[1]) // (4 * RET_HEADS)
    inv_freq = (ROPE_BASE ** (-jnp.arange(ret_half, dtype=F32) / ret_half))[None, :]
    depth = gla_norm.shape[0] + ret_norm.shape[0]
    assert depth >= 1
    hn = cos = sin = None
    for i in range(depth):
        j = i // 2
        final = i == depth - 1
        if final:
            out_norm_g = final_norm
        else:
            out_norm_g = ret_norm[j] if i % 2 == 0 else gla_norm[j + 1]
        if i % 2 == 0:
            out = _gla_layer(h, batch, seq, gla_norm[j], gla_w_in[j], gla_w_a1[j], gla_w_a2[j], gla_b_a[j],
                             gla_head_g[j], gla_w_out[j], out_norm_g[None, :], pos, inv_freq, final)
        else:
            out = _ret_layer(h, hn, cos, sin, batch, seq, ret_w_in[j], ret_gn_g[j], ret_gn_b[j], ret_w_out[j],
                             out_norm_g[None, :], pos, inv_freq, final)
        if final:
            h = out
        else:
            h, hn, cos, sin = out
    return h.reshape(batch, seq, d)
```

```python
import functools
import math

import jax
import jax.numpy as jnp
from jax import lax
from jax.experimental import pallas as pl
from jax.experimental.pallas import tpu as pltpu

F32 = jnp.float32
BF16 = jnp.bfloat16

EPS = 1e-6
GLA_HEADS = 4
GLA_GATE_RANK_PAD = 128
GLA_GATE_TEMP = 16.0
RET_HEADS = 8
ROPE_BASE = 10000.0

CHUNK = 256
HEADS_PER_STEP = 2
GLA_CHUNKS_PER_STEP = 2
RET_CHUNKS_PER_STEP = 4
FAST_DECAY_LIMIT = 60.0
VMEM_LIMIT_BYTES = 56 * 1024 * 1024

_NT = (((1,), (1,)), ((), ()))
_TN = (((0,), (0,)), ((), ()))


def _dot(a, b):
    return jnp.dot(a, b, preferred_element_type=F32)


def _rms_rows(x, g):
    ms = jnp.mean(x * x, axis=-1, keepdims=True)
    return x * lax.rsqrt(ms + EPS) * g


def _silu(g):
    hg = 0.5 * g
    return hg + hg * jnp.tanh(hg)


def _split3(a):
    hi = a.astype(BF16)
    r1 = a - hi.astype(F32)
    mid = r1.astype(BF16)
    lo = (r1 - mid.astype(F32)).astype(BF16)
    return hi, mid, lo


def _causal(c):
    row = lax.broadcasted_iota(jnp.int32, (c, c), 0)
    col = lax.broadcasted_iota(jnp.int32, (c, c), 1)
    return row >= col


def _gla_in_kernel(x_ref, g_ref, w_ref, wa1_ref, wa2_ref, ba_ref, proj_ref, la_ref, least_ref, hn_ref, z_ref):
    j = pl.program_id(1)

    def project():
        proj_ref[...] = _dot(hn_ref[...], w_ref[...]).astype(BF16)

    @pl.when(j == 0)
    def _():
        hn = _rms_rows(x_ref[...], g_ref[...]).astype(BF16)
        hn_ref[...] = hn
        z_ref[...] = _dot(hn, wa1_ref[...]).astype(BF16)
        project()

    @pl.when(j == 1)
    def _():
        xa = _dot(z_ref[...], wa2_ref[...]) + ba_ref[...]
        la = (jnp.minimum(xa, 0.0) - jnp.log(1.0 + jnp.exp(-jnp.abs(xa)))) * (1.0 / GLA_GATE_TEMP)
        la_ref[...] = la
        least = jnp.sum(la[:CHUNK], axis=0, keepdims=True)
        for c0 in range(CHUNK, la.shape[0], CHUNK):
            least = jnp.minimum(least, jnp.sum(la[c0:c0 + CHUNK], axis=0, keepdims=True))
        least_ref[...] = jnp.broadcast_to(least, least_ref.shape)
        project()

    @pl.when(j > 1)
    def _():
        project()


def _gla_in_proj(x2, g, w, wa1, wa2, ba, *, tm, tn):
    m, d = x2.shape
    n = w.shape[1]
    dk = wa2.shape[1]
    assert n // tn >= 2
    assert tm % CHUNK == 0
    return pl.pallas_call(
        _gla_in_kernel,
        grid=(m // tm, n // tn),
        in_specs=[
            pl.BlockSpec((tm, d), lambda i, j: (i, 0)),
            pl.BlockSpec((1, d), lambda i, j: (0, 0)),
            pl.BlockSpec((d, tn), lambda i, j: (0, j)),
            pl.BlockSpec((d, GLA_GATE_RANK_PAD), lambda i, j: (0, 0)),
            pl.BlockSpec((GLA_GATE_RANK_PAD, dk), lambda i, j: (0, 0)),
            pl.BlockSpec((1, dk), lambda i, j: (0, 0)),
        ],
        out_specs=[
            pl.BlockSpec((tm, tn), lambda i, j: (i, j)),
            pl.BlockSpec((tm, dk), lambda i, j: (i, 0)),
            pl.BlockSpec((8, dk), lambda i, j: (i, 0)),
        ],
        out_shape=[
            jax.ShapeDtypeStruct((m, n), BF16),
            jax.ShapeDtypeStruct((m, dk), F32),
            jax.ShapeDtypeStruct((m // tm * 8, dk), F32),
        ],
        scratch_shapes=[pltpu.VMEM((tm, d), BF16), pltpu.VMEM((tm, GLA_GATE_RANK_PAD), BF16)],
        compiler_params=pltpu.CompilerParams(
            dimension_semantics=("arbitrary", "arbitrary"), vmem_limit_bytes=VMEM_LIMIT_BYTES),
        name="gla_in_proj",
    )(x2, g, w, wa1, wa2, ba)


def _gla_rec_kernel(q_ref, k_ref, v_ref, la_ref, least_ref, hg_ref, o_ref,
                    state_ref, ltri_ref, cum_ref, kf_ref, sc_ref, *, q_scale):
    c = CHUNK
    n_heads, dk, dv = state_ref.shape
    n_chunks = q_ref.shape[0] // c

    @pl.when(pl.program_id(2) == 0)
    def _():
        state_ref[...] = jnp.zeros_like(state_ref)
        ltri_ref[...] = jnp.where(_causal(c), 1.0, 0.0).astype(BF16)

    ones = jnp.ones((c, 128), BF16)

    def chunk(i, hd, direct):
        rows = pl.ds(i * c, c)
        kcols = slice(hd * dk, (hd + 1) * dk)
        vcols = slice(hd * dv, (hd + 1) * dv)
        ltri = ltri_ref[...]
        a_hi, a_mid, a_lo = _split3(la_ref[rows, kcols])
        cum = _dot(ltri, a_hi) + _dot(ltri, a_mid) + _dot(ltri, a_lo)
        tot_col = (lax.dot_general(a_hi, ones, _TN, preferred_element_type=F32)
                   + lax.dot_general(a_mid, ones, _TN, preferred_element_type=F32)
                   + lax.dot_general(a_lo, ones, _TN, preferred_element_type=F32))
        total = cum[c - 1:c, :]

        qf = q_ref[rows, kcols].astype(F32) * q_scale
        kf = k_ref[rows, kcols].astype(F32)
        v = v_ref[rows, vcols]
        qs = (qf * jnp.exp(cum)).astype(BF16)
        ks_f = kf * jnp.exp(total - cum)
        ks = ks_f.astype(BF16)

        if direct:
            kn = (ks_f * jnp.exp(-total)).astype(BF16)
            s = lax.dot_general(qs, kn, _NT, preferred_element_type=F32)
        else:
            cum_ref[...] = cum
            kf_ref[...] = kf
            sc_ref[...] = jnp.zeros_like(sc_ref)
            lane = lax.broadcasted_iota(jnp.int32, (1, c), 1)

            def column(s, carry):
                d = jnp.exp(jnp.minimum(cum_ref[...] - cum_ref[pl.ds(s, 1), :], 0.0))
                colv = jnp.sum(qf * kf_ref[pl.ds(s, 1), :] * d, axis=1, keepdims=True)
                sc_ref[...] += colv * jnp.where(lane == s, 1.0, 0.0)
                return carry

            lax.fori_loop(0, c, column, 0)
            s = sc_ref[...]
        p = jnp.where(_causal(c), s, 0.0).astype(BF16)

        st = state_ref[hd]
        o = _dot(jnp.concatenate([qs, p], axis=1), jnp.concatenate([st.astype(BF16), v], axis=0))
        decay = jnp.exp(tot_col)
        decay = jnp.concatenate([decay] * (dv // 128), axis=1)
        state_ref[hd] = st * decay + lax.dot_general(ks, v, _TN, preferred_element_type=F32)

        ms = jnp.mean(o * o, axis=-1, keepdims=True)
        on = o * lax.rsqrt(ms + EPS)
        o_ref[rows, vcols] = (on * hg_ref[:, vcols]).astype(BF16)

    def step(direct):
        for i in range(n_chunks):
            for hd in range(n_heads):
                chunk(i, hd, direct)

    direct_ok = jnp.min(least_ref[...]) >= -FAST_DECAY_LIMIT

    @pl.when(direct_ok)
    def _():
        step(True)

    @pl.when(jnp.logical_not(direct_ok))
    def _():
        step(False)


def _gla_recurrence(proj, la, least, head_g, *, batch, seq, dk_total, dv_total, least_rows):
    m = proj.shape[0]
    hps = HEADS_PER_STEP
    hk = dk_total // GLA_HEADS
    hv = dv_total // GLA_HEADS
    t_rows = CHUNK * GLA_CHUNKS_PER_STEP
    nt = seq // t_rows
    k_off = dk_total // (hps * hk)
    v_off = 2 * dk_total // (hps * hv)
    rows = lambda b, h, t: b * nt + t
    assert least_rows % t_rows == 0
    return pl.pallas_call(
        functools.partial(_gla_rec_kernel, q_scale=float(hk) ** -0.5),
        grid=(batch, GLA_HEADS // hps, nt),
        in_specs=[
            pl.BlockSpec((t_rows, hps * hk), lambda b, h, t: (rows(b, h, t), h)),
            pl.BlockSpec((t_rows, hps * hk), lambda b, h, t: (rows(b, h, t), k_off + h)),
            pl.BlockSpec((t_rows, hps * hv), lambda b, h, t: (rows(b, h, t), v_off + h)),
            pl.BlockSpec((t_rows, hps * hk), lambda b, h, t: (rows(b, h, t), h)),
            pl.BlockSpec((8, hps * hk), lambda b, h, t: (rows(b, h, t) // (least_rows // t_rows), h)),
            pl.BlockSpec((1, hps * hv), lambda b, h, t: (0, h)),
        ],
        out_specs=pl.BlockSpec((t_rows, hps * hv), lambda b, h, t: (rows(b, h, t), h)),
        out_shape=jax.ShapeDtypeStruct((m, dv_total), BF16),
        scratch_shapes=[
            pltpu.VMEM((hps, hk, hv), F32),
            pltpu.VMEM((CHUNK, CHUNK), BF16),
            pltpu.VMEM((CHUNK, hk), F32),
            pltpu.VMEM((CHUNK, hk), F32),
            pltpu.VMEM((CHUNK, CHUNK), F32),
        ],
        compiler_params=pltpu.CompilerParams(
            dimension_semantics=("arbitrary", "arbitrary", "arbitrary"), vmem_limit_bytes=VMEM_LIMIT_BYTES),
        name="gla_recurrence",
    )(proj, proj, proj, la, least, head_g)


def _gated_out(a_ref, gate_refs, w_ref, res_ref):
    gw = gate_refs[0].shape[1]
    y = res_ref[...]
    for b, g_ref in enumerate(gate_refs):
        cols = slice(b * gw, (b + 1) * gw)
        gated = (a_ref[:, cols].astype(F32) * _silu(g_ref[...].astype(F32))).astype(BF16)
        y = y + _dot(gated, w_ref[cols, :])
    return y


def _out_proj_final_kernel(a_ref, *refs):
    *gate_refs, w_ref, res_ref, g_ref, o_ref = refs
    o_ref[...] = _rms_rows(_gated_out(a_ref, gate_refs, w_ref, res_ref), g_ref[...])


def _out_proj_next_kernel(a_ref, *refs):
    *gate_refs, w_ref, res_ref, g_ref, pos_ref, invf_ref, o_ref, hn_ref, cos_ref, sin_ref = refs
    y = _gated_out(a_ref, gate_refs, w_ref, res_ref)
    o_ref[...] = y
    hn_ref[...] = _rms_rows(y, g_ref[...]).astype(BF16)
    ang = pos_ref[...].astype(F32) * invf_ref[...]
    cos_ref[...] = jnp.cos(ang)
    sin_ref[...] = jnp.sin(ang)


def _out_proj(a, proj, gate_col, w, res, norm_g, pos=None, inv_freq=None, *, final, tm):
    m, k = a.shape
    n = w.shape[1]
    gw = math.gcd(gate_col, k)
    gate_specs = [pl.BlockSpec((tm, gw), functools.partial(lambda i, c: (i, c), c=gate_col // gw + b))
                  for b in range(k // gw)]
    row_spec = pl.BlockSpec((tm, n), lambda i: (i, 0))
    in_specs = [pl.BlockSpec((tm, k), lambda i: (i, 0))] + gate_specs + [
        pl.BlockSpec((k, n), lambda i: (0, 0), pipeline_mode=pl.Buffered(1)),
        row_spec,
        pl.BlockSpec((1, n), lambda i: (0, 0)),
    ]
    args = (a,) + (proj,) * len(gate_specs) + (w, res, norm_g)
    y_shape = jax.ShapeDtypeStruct((m, n), F32)
    if final:
        out_specs, out_shape = row_spec, y_shape
    else:
        half = inv_freq.shape[1]
        in_specs += [pl.BlockSpec((tm, 1), lambda i: (i, 0)), pl.BlockSpec((1, half), lambda i: (0, 0))]
        args += (pos, inv_freq)
        rot_spec = pl.BlockSpec((tm, half), lambda i: (i, 0))
        rot_shape = jax.ShapeDtypeStruct((m, half), F32)
        out_specs = [row_spec, row_spec, rot_spec, rot_spec]
        out_shape = [y_shape, jax.ShapeDtypeStruct((m, n), BF16), rot_shape, rot_shape]
    return pl.pallas_call(
        _out_proj_final_kernel if final else _out_proj_next_kernel,
        grid=(m // tm,),
        in_specs=in_specs,
        out_specs=out_specs,
        out_shape=out_shape,
        compiler_params=pltpu.CompilerParams(
            dimension_semantics=("arbitrary",), vmem_limit_bytes=VMEM_LIMIT_BYTES),
        name="out_proj_final" if final else "out_proj_next",
    )(*args)


def _ret_in_kernel(hn_ref, cos_ref, sin_ref, fac_ref, w_ref, proj_ref, *, hk, k_scale):
    j = pl.program_id(1)
    half = hk // 2
    n_heads = w_ref.shape[1] // hk

    def rotary(first_fac, scale):
        acc = _dot(hn_ref[...], w_ref[...])
        for h in range(n_heads):
            fac = fac_ref[:, first_fac + h:first_fac + h + 1] * scale
            cos = cos_ref[...] * fac
            sin = sin_ref[...] * fac
            t1 = acc[:, h * hk:h * hk + half]
            t2 = acc[:, h * hk + half:(h + 1) * hk]
            proj_ref[:, h * hk:h * hk + half] = (t1 * cos - t2 * sin).astype(BF16)
            proj_ref[:, h * hk + half:(h + 1) * hk] = (t2 * cos + t1 * sin).astype(BF16)

    @pl.when(j == 0)
    def _():
        rotary(0, 1.0)

    @pl.when(j == 1)
    def _():
        rotary(n_heads, k_scale)

    @pl.when(j > 1)
    def _():
        proj_ref[...] = _dot(hn_ref[...], w_ref[...]).astype(BF16)


def _ret_in_proj(hn, cos, sin, fac, w, *, dk_total, tm, tn):
    m, d = hn.shape
    n = w.shape[1]
    hk = dk_total // RET_HEADS
    half = hk // 2
    assert tn == dk_total
    kern = functools.partial(_ret_in_kernel, hk=hk, k_scale=float(hk) ** -0.5)
    return pl.pallas_call(
        kern,
        grid=(m // tm, n // tn),
        in_specs=[
            pl.BlockSpec((tm, d), lambda i, j: (i, 0)),
            pl.BlockSpec((tm, half), lambda i, j: (i, 0)),
            pl.BlockSpec((tm, half), lambda i, j: (i, 0)),
            pl.BlockSpec(fac.shape, lambda i, j: (0, 0)),
            pl.BlockSpec((d, tn), lambda i, j: (0, j)),
        ],
        out_specs=pl.BlockSpec((tm, tn), lambda i, j: (i, j)),
        out_shape=jax.ShapeDtypeStruct((m, n), BF16),
        compiler_params=pltpu.CompilerParams(
            dimension_semantics=("arbitrary", "arbitrary"), vmem_limit_bytes=VMEM_LIMIT_BYTES),
        name="ret_in_proj",
    )(hn, cos, sin, fac, w)


def _ret_rec_kernel(q_ref, k_ref, v_ref, lg_ref, gng_ref, gnb_ref, o_ref, state_ref):
    c = CHUNK
    n_heads, dk, dv = state_ref.shape

    @pl.when(pl.program_id(2) == 0)
    def _():
        state_ref[...] = jnp.zeros_like(state_ref)

    def chunk(i, hd):
        rows = pl.ds(i * c, c)
        kcols = slice(hd * dk, (hd + 1) * dk)
        vcols = slice(hd * dv, (hd + 1) * dv)
        q = q_ref[rows, kcols]
        k = k_ref[rows, kcols]
        v = v_ref[rows, vcols]
        s = lax.dot_general(q, k, _NT, preferred_element_type=F32)
        p = jnp.where(_causal(c), s, 0.0).astype(BF16)
        st = state_ref[hd]
        o = _dot(jnp.concatenate([p, q], axis=1), jnp.concatenate([v, st.astype(BF16)], axis=0))
        state_ref[hd] = ((st + lax.dot_general(k, v, _TN, preferred_element_type=F32))
                         * jnp.exp(lg_ref[hd, :, :dv] * float(c)))

        mu = jnp.mean(o, axis=-1, keepdims=True)
        d = o - mu
        var = jnp.mean(d * d, axis=-1, keepdims=True)
        on = d * lax.rsqrt(var + EPS)
        o_ref[rows, vcols] = (on * gng_ref[:, vcols] + gnb_ref[:, vcols]).astype(BF16)

    for i in range(q_ref.shape[0] // c):
        for hd in range(n_heads):
            chunk(i, hd)


def _ret_recurrence(proj, log_gamma, gn_g, gn_b, *, batch, seq, dk_total, dv_total):
    m = proj.shape[0]
    hps = HEADS_PER_STEP
    hk = dk_total // RET_HEADS
    hv = dv_total // RET_HEADS
    t_rows = CHUNK * RET_CHUNKS_PER_STEP
    nt = seq // t_rows
    width = log_gamma.shape[-1]
    k_off = dk_total // (hps * hk)
    v_off = 2 * dk_total // (hps * hv)
    rows = lambda b, h, t: b * nt + t
    return pl.pallas_call(
        _ret_rec_kernel,
        grid=(batch, RET_HEADS // hps, nt),
        in_specs=[
            pl.BlockSpec((t_rows, hps * hk), lambda b, h, t: (rows(b, h, t), h)),
            pl.BlockSpec((t_rows, hps * hk), lambda b, h, t: (rows(b, h, t), k_off + h)),
            pl.BlockSpec((t_rows, hps * hv), lambda b, h, t: (rows(b, h, t), v_off + h)),
            pl.BlockSpec((hps, 1, width), lambda b, h, t: (h, 0, 0)),
            pl.BlockSpec((1, hps * hv), lambda b, h, t: (0, h)),
            pl.BlockSpec((1, hps * hv), lambda b, h, t: (0, h)),
        ],
        out_specs=pl.BlockSpec((t_rows, hps * hv), lambda b, h, t: (rows(b, h, t), h)),
        out_shape=jax.ShapeDtypeStruct((m, dv_total), BF16),
        scratch_shapes=[pltpu.VMEM((hps, hk, hv), F32)],
        compiler_params=pltpu.CompilerParams(
            dimension_semantics=("arbitrary", "arbitrary", "arbitrary"), vmem_limit_bytes=VMEM_LIMIT_BYTES),
        name="ret_recurrence",
    )(proj, proj, proj, log_gamma, gn_g, gn_b)


GLA_IN_PROJ_TILE = (1024, 1024)
RET_IN_PROJ_TILE = (1024, 2048)
OUT_PROJ_TM = {True: 512, False: 256}


def _gla_layer(h, batch, seq, norm_g, w_in, w_a1, w_a2, b_a, head_g, w_out, out_norm_g, pos, inv_freq, final):
    d = h.shape[1]
    dk_total = w_a2.shape[1]
    dv_total = w_out.shape[0]
    rank = w_a1.shape[1]
    wa1 = jnp.zeros((d, GLA_GATE_RANK_PAD), BF16).at[:, :rank].set(w_a1.astype(BF16))
    wa2 = jnp.zeros((GLA_GATE_RANK_PAD, dk_total), BF16).at[:rank, :].set(w_a2.astype(BF16))
    tm, tn = GLA_IN_PROJ_TILE
    proj, la, least = _gla_in_proj(h, norm_g[None, :], w_in.astype(BF16), wa1, wa2, b_a[None, :], tm=tm, tn=tn)
    o = _gla_recurrence(proj, la, least, head_g[None, :], batch=batch, seq=seq, dk_total=dk_total,
                        dv_total=dv_total, least_rows=tm)
    return _out_proj(o, proj, 2 * dk_total + dv_total, w_out.astype(BF16), h, out_norm_g, pos, inv_freq,
                     final=final, tm=OUT_PROJ_TM[final])


def _ret_layer(h, hn, cos, sin, batch, seq, w_in, gn_g, gn_b, w_out, out_norm_g, pos, inv_freq, final):
    dv_total = w_out.shape[0]
    dk_total = (w_in.shape[1] - 2 * dv_total) // 2
    hk = dk_total // RET_HEADS
    log_gamma = jnp.log1p(-jnp.exp2(-5.0 - jnp.arange(RET_HEADS, dtype=F32)))
    tm, tn = RET_IN_PROJ_TILE
    steps = (jnp.arange(tm, dtype=F32) % CHUNK + 1.0)[:, None] * log_gamma[None, :]
    fac = jnp.concatenate([jnp.exp(steps), jnp.exp(-steps)], axis=1)
    width = max(CHUNK, hk, dv_total // RET_HEADS)
    log_gamma = jnp.broadcast_to(log_gamma[:, None, None], (RET_HEADS, 1, width))
    proj = _ret_in_proj(hn, cos, sin, fac, w_in.astype(BF16), dk_total=dk_total, tm=tm, tn=tn)
    o = _ret_recurrence(proj, log_gamma, gn_g[None, :], gn_b[None, :],
                        batch=batch, seq=seq, dk_total=dk_total, dv_total=dv_total)
    return _out_proj(o, proj, 2 * dk_total + dv_total, w_out.astype(BF16), h, out_norm_g, pos, inv_freq,
                     final=final, tm=OUT_PROJ_TM[final])


def kernel(x, positions, gla_norm, gla_w_in, gla_w_a1, gla_w_a2, gla_b_a, gla_head_g, gla_w_out,
           ret_norm, ret_w_in, ret_gn_g, ret_gn_b, ret_w_out, final_norm):
    batch, seq, d = x.shape
    m = batch * seq
    h = x.reshape(m, d)
    pos = positions.reshape(m, 1)
    ret_half = (ret_w_in.shape[2] - 2 * ret_w_out.shape[1]) // (4 * RET_HEADS)
    inv_freq = (ROPE_BASE ** (-jnp.arange(ret_half, dtype=F32) / ret_half))[None, :]
    depth = gla_norm.shape[0] + ret_norm.shape[0]
    assert depth >= 1
    hn = cos = sin = None
    for i in range(depth):
        j = i // 2
        final = i == depth - 1
        if final:
            out_norm_g = final_norm
        else:
            out_norm_g = ret_norm[j] if i % 2 == 0 else gla_norm[j + 1]
        if i % 2 == 0:
            out = _gla_layer(h, batch, seq, gla_norm[j], gla_w_in[j], gla_w_a1[j], gla_w_a2[j], gla_b_a[j],
                             gla_head_g[j], gla_w_out[j], out_norm_g[None, :], pos, inv_freq, final)
        else:
            out = _ret_layer(h, hn, cos, sin, batch, seq, ret_w_in[j], ret_gn_g[j], ret_gn_b[j], ret_w_out[j],
                             out_norm_g[None, :], pos, inv_freq, final)
        if final:
            h = out
        else:
            h, hn, cos, sin = out
    return h.reshape(batch, seq, d)
```

```python
import functools
import math

import jax
import jax.numpy as jnp
from jax import lax
from jax.experimental import pallas as pl
from jax.experimental.pallas import tpu as pltpu

F32 = jnp.float32
BF16 = jnp.bfloat16

EPS = 1e-6
GLA_HEADS = 4
GLA_GATE_RANK_PAD = 128
GLA_GATE_TEMP = 16.0
RET_HEADS = 8
ROPE_BASE = 10000.0

CHUNK = 256
HEADS_PER_STEP = 2
GLA_CHUNKS_PER_STEP = 2
RET_CHUNKS_PER_STEP = 8
FAST_DECAY_LIMIT = 60.0
VMEM_LIMIT_BYTES = 56 * 1024 * 1024
OUT_PROJ_VMEM_LIMIT_BYTES = 60 * 1024 * 1024

_NT = (((1,), (1,)), ((), ()))
_TN = (((0,), (0,)), ((), ()))


def _dot(a, b):
    return jnp.dot(a, b, preferred_element_type=F32)


def _rms_rows(x, g):
    ms = jnp.mean(x * x, axis=-1, keepdims=True)
    return x * lax.rsqrt(ms + EPS) * g


def _silu(g):
    hg = 0.5 * g
    return hg + hg * jnp.tanh(hg)


def _split3(a):
    hi = a.astype(BF16)
    r1 = a - hi.astype(F32)
    mid = r1.astype(BF16)
    lo = (r1 - mid.astype(F32)).astype(BF16)
    return hi, mid, lo


def _causal(c):
    row = lax.broadcasted_iota(jnp.int32, (c, c), 0)
    col = lax.broadcasted_iota(jnp.int32, (c, c), 1)
    return row >= col


def _gla_in_kernel(x_ref, g_ref, w_ref, wa1_ref, wa2_ref, ba_ref, proj_ref, la_ref, least_ref, hn_ref, z_ref):
    j = pl.program_id(1)

    def project():
        proj_ref[...] = _dot(hn_ref[...], w_ref[...]).astype(BF16)

    @pl.when(j == 0)
    def _():
        hn = _rms_rows(x_ref[...], g_ref[...]).astype(BF16)
        hn_ref[...] = hn
        z_ref[...] = _dot(hn, wa1_ref[...]).astype(BF16)
        project()

    @pl.when(j == 1)
    def _():
        xa = _dot(z_ref[...], wa2_ref[...]) + ba_ref[...]
        la = (jnp.minimum(xa, 0.0) - jnp.log(1.0 + jnp.exp(-jnp.abs(xa)))) * (1.0 / GLA_GATE_TEMP)
        la_ref[...] = la
        least = jnp.sum(la[:CHUNK], axis=0, keepdims=True)
        for c0 in range(CHUNK, la.shape[0], CHUNK):
            least = jnp.minimum(least, jnp.sum(la[c0:c0 + CHUNK], axis=0, keepdims=True))
        least_ref[...] = jnp.broadcast_to(least, least_ref.shape)
        project()

    @pl.when(j > 1)
    def _():
        project()


def _gla_in_proj(x2, g, w, wa1, wa2, ba, *, tm, tn):
    m, d = x2.shape
    n = w.shape[1]
    dk = wa2.shape[1]
    assert n // tn >= 2
    assert tm % CHUNK == 0
    return pl.pallas_call(
        _gla_in_kernel,
        grid=(m // tm, n // tn),
        in_specs=[
            pl.BlockSpec((tm, d), lambda i, j: (i, 0)),
            pl.BlockSpec((1, d), lambda i, j: (0, 0)),
            pl.BlockSpec((d, tn), lambda i, j: (0, j)),
            pl.BlockSpec((d, GLA_GATE_RANK_PAD), lambda i, j: (0, 0)),
            pl.BlockSpec((GLA_GATE_RANK_PAD, dk), lambda i, j: (0, 0)),
            pl.BlockSpec((1, dk), lambda i, j: (0, 0)),
        ],
        out_specs=[
            pl.BlockSpec((tm, tn), lambda i, j: (i, j)),
            pl.BlockSpec((tm, dk), lambda i, j: (i, 0)),
            pl.BlockSpec((8, dk), lambda i, j: (i, 0)),
        ],
        out_shape=[
            jax.ShapeDtypeStruct((m, n), BF16),
            jax.ShapeDtypeStruct((m, dk), F32),
            jax.ShapeDtypeStruct((m // tm * 8, dk), F32),
        ],
        scratch_shapes=[pltpu.VMEM((tm, d), BF16), pltpu.VMEM((tm, GLA_GATE_RANK_PAD), BF16)],
        compiler_params=pltpu.CompilerParams(
            dimension_semantics=("arbitrary", "arbitrary"), vmem_limit_bytes=VMEM_LIMIT_BYTES),
        name="gla_in_proj",
    )(x2, g, w, wa1, wa2, ba)


def _gla_rec_kernel(q_ref, k_ref, v_ref, la_ref, least_ref, hg_ref, o_ref,
                    state_ref, ltri_ref, cum_ref, kf_ref, sc_ref, *, q_scale):
    c = CHUNK
    n_heads, dk, dv = state_ref.shape
    n_chunks = q_ref.shape[0] // c

    @pl.when(pl.program_id(2) == 0)
    def _():
        state_ref[...] = jnp.zeros_like(state_ref)
        ltri_ref[...] = jnp.where(_causal(c), 1.0, 0.0).astype(BF16)

    ones = jnp.ones((c, 128), BF16)

    def chunk(i, hd, direct):
        rows = pl.ds(i * c, c)
        kcols = slice(hd * dk, (hd + 1) * dk)
        vcols = slice(hd * dv, (hd + 1) * dv)
        ltri = ltri_ref[...]
        a_hi, a_mid, a_lo = _split3(la_ref[rows, kcols])
        cum = _dot(ltri, a_hi) + _dot(ltri, a_mid) + _dot(ltri, a_lo)
        tot_col = (lax.dot_general(a_hi, ones, _TN, preferred_element_type=F32)
                   + lax.dot_general(a_mid, ones, _TN, preferred_element_type=F32)
                   + lax.dot_general(a_lo, ones, _TN, preferred_element_type=F32))
        total = cum[c - 1:c, :]

        qf = q_ref[rows, kcols].astype(F32) * q_scale
        kf = k_ref[rows, kcols].astype(F32)
        v = v_ref[rows, vcols]
        qs = (qf * jnp.exp(cum)).astype(BF16)
        ks_f = kf * jnp.exp(total - cum)
        ks = ks_f.astype(BF16)

        if direct:
            kn = (ks_f * jnp.exp(-total)).astype(BF16)
            s = lax.dot_general(qs, kn, _NT, preferred_element_type=F32)
        else:
            cum_ref[...] = cum
            kf_ref[...] = kf
            sc_ref[...] = jnp.zeros_like(sc_ref)
            lane = lax.broadcasted_iota(jnp.int32, (1, c), 1)

            def column(s, carry):
                d = jnp.exp(jnp.minimum(cum_ref[...] - cum_ref[pl.ds(s, 1), :], 0.0))
                colv = jnp.sum(qf * kf_ref[pl.ds(s, 1), :] * d, axis=1, keepdims=True)
                sc_ref[...] += colv * jnp.where(lane == s, 1.0, 0.0)
                return carry

            lax.fori_loop(0, c, column, 0)
            s = sc_ref[...]
        p = jnp.where(_causal(c), s, 0.0).astype(BF16)

        st = state_ref[hd]
        o = _dot(qs, st.astype(BF16)) + _dot(p, v)
        decay = jnp.exp(tot_col)
        decay = jnp.concatenate([decay] * (dv // 128), axis=1)
        state_ref[hd] = st * decay + lax.dot_general(ks, v, _TN, preferred_element_type=F32)

        ms = jnp.mean(o * o, axis=-1, keepdims=True)
        on = o * lax.rsqrt(ms + EPS)
        o_ref[rows, vcols] = (on * hg_ref[:, vcols]).astype(BF16)

    def step(direct):
        for i in range(n_chunks):
            for hd in range(n_heads):
                chunk(i, hd, direct)

    direct_ok = jnp.min(least_ref[...]) >= -FAST_DECAY_LIMIT

    @pl.when(direct_ok)
    def _():
        step(True)

    @pl.when(jnp.logical_not(direct_ok))
    def _():
        step(False)


def _gla_recurrence(proj, la, least, head_g, *, batch, seq, dk_total, dv_total, least_rows):
    m = proj.shape[0]
    hps = HEADS_PER_STEP
    hk = dk_total // GLA_HEADS
    hv = dv_total // GLA_HEADS
    t_rows = CHUNK * GLA_CHUNKS_PER_STEP
    nt = seq // t_rows
    k_off = dk_total // (hps * hk)
    v_off = 2 * dk_total // (hps * hv)
    rows = lambda b, h, t: b * nt + t
    assert least_rows % t_rows == 0
    return pl.pallas_call(
        functools.partial(_gla_rec_kernel, q_scale=float(hk) ** -0.5),
        grid=(batch, GLA_HEADS // hps, nt),
        in_specs=[
            pl.BlockSpec((t_rows, hps * hk), lambda b, h, t: (rows(b, h, t), h)),
            pl.BlockSpec((t_rows, hps * hk), lambda b, h, t: (rows(b, h, t), k_off + h)),
            pl.BlockSpec((t_rows, hps * hv), lambda b, h, t: (rows(b, h, t), v_off + h)),
            pl.BlockSpec((t_rows, hps * hk), lambda b, h, t: (rows(b, h, t), h)),
            pl.BlockSpec((8, hps * hk), lambda b, h, t: (rows(b, h, t) // (least_rows // t_rows), h)),
            pl.BlockSpec((1, hps * hv), lambda b, h, t: (0, h)),
        ],
        out_specs=pl.BlockSpec((t_rows, hps * hv), lambda b, h, t: (rows(b, h, t), h)),
        out_shape=jax.ShapeDtypeStruct((m, dv_total), BF16),
        scratch_shapes=[
            pltpu.VMEM((hps, hk, hv), F32),
            pltpu.VMEM((CHUNK, CHUNK), BF16),
            pltpu.VMEM((CHUNK, hk), F32),
            pltpu.VMEM((CHUNK, hk), F32),
            pltpu.VMEM((CHUNK, CHUNK), F32),
        ],
        compiler_params=pltpu.CompilerParams(
            dimension_semantics=("arbitrary", "arbitrary", "arbitrary"), vmem_limit_bytes=VMEM_LIMIT_BYTES),
        name="gla_recurrence",
    )(proj, proj, proj, la, least, head_g)


def _gated_out(a_ref, gate_refs, w_ref, res_ref):
    gw = gate_refs[0].shape[1]
    y = res_ref[...]
    for b, g_ref in enumerate(gate_refs):
        cols = slice(b * gw, (b + 1) * gw)
        gated = (a_ref[:, cols].astype(F32) * _silu(g_ref[...].astype(F32))).astype(BF16)
        y = y + _dot(gated, w_ref[cols, :])
    return y


def _out_proj_final_kernel(a_ref, *refs):
    *gate_refs, w_ref, res_ref, g_ref, o_ref = refs
    o_ref[...] = _rms_rows(_gated_out(a_ref, gate_refs, w_ref, res_ref), g_ref[...])


def _out_proj_next_kernel(a_ref, *refs):
    *gate_refs, w_ref, res_ref, g_ref, pos_ref, invf_ref, o_ref, hn_ref, cos_ref, sin_ref = refs
    y = _gated_out(a_ref, gate_refs, w_ref, res_ref)
    o_ref[...] = y
    hn_ref[...] = _rms_rows(y, g_ref[...]).astype(BF16)
    ang = pos_ref[...].astype(F32) * invf_ref[...]
    cos_ref[...] = jnp.cos(ang)
    sin_ref[...] = jnp.sin(ang)


def _out_proj(a, proj, gate_col, w, res, norm_g, pos=None, inv_freq=None, *, final, tm):
    m, k = a.shape
    n = w.shape[1]
    gw = math.gcd(gate_col, k)
    gate_specs = [pl.BlockSpec((tm, gw), functools.partial(lambda i, c: (i, c), c=gate_col // gw + b))
                  for b in range(k // gw)]
    row_spec = pl.BlockSpec((tm, n), lambda i: (i, 0))
    in_specs = [pl.BlockSpec((tm, k), lambda i: (i, 0))] + gate_specs + [
        pl.BlockSpec((k, n), lambda i: (0, 0), pipeline_mode=pl.Buffered(1)),
        row_spec,
        pl.BlockSpec((1, n), lambda i: (0, 0)),
    ]
    args = (a,) + (proj,) * len(gate_specs) + (w, res, norm_g)
    y_shape = jax.ShapeDtypeStruct((m, n), F32)
    if final:
        out_specs, out_shape = row_spec, y_shape
    else:
        half = inv_freq.shape[1]
        in_specs += [pl.BlockSpec((tm, 1), lambda i: (i, 0)), pl.BlockSpec((1, half), lambda i: (0, 0))]
        args += (pos, inv_freq)
        rot_spec = pl.BlockSpec((tm, half), lambda i: (i, 0))
        rot_shape = jax.ShapeDtypeStruct((m, half), F32)
        out_specs = [row_spec, row_spec, rot_spec, rot_spec]
        out_shape = [y_shape, jax.ShapeDtypeStruct((m, n), BF16), rot_shape, rot_shape]
    return pl.pallas_call(
        _out_proj_final_kernel if final else _out_proj_next_kernel,
        grid=(m // tm,),
        in_specs=in_specs,
        out_specs=out_specs,
        out_shape=out_shape,
        compiler_params=pltpu.CompilerParams(
            dimension_semantics=("arbitrary",), vmem_limit_bytes=OUT_PROJ_VMEM_LIMIT_BYTES),
        name="out_proj_final" if final else "out_proj_next",
    )(*args)


def _ret_in_kernel(hn_ref, cos_ref, sin_ref, w_ref, proj_ref, *, n_q_blocks, n_k_blocks, hk, k_scale):
    j = pl.program_id(1)
    half = hk // 2

    @pl.when(j < n_q_blocks + n_k_blocks)
    def _():
        scale = jnp.where(j >= n_q_blocks, k_scale, 1.0).astype(F32)
        cos = cos_ref[...] * scale
        sin = sin_ref[...] * scale
        acc = _dot(hn_ref[...], w_ref[...])
        for h in range(acc.shape[1] // hk):
            t1 = acc[:, h * hk:h * hk + half]
            t2 = acc[:, h * hk + half:(h + 1) * hk]
            proj_ref[:, h * hk:h * hk + half] = (t1 * cos - t2 * sin).astype(BF16)
            proj_ref[:, h * hk + half:(h + 1) * hk] = (t2 * cos + t1 * sin).astype(BF16)

    @pl.when(j >= n_q_blocks + n_k_blocks)
    def _():
        proj_ref[...] = _dot(hn_ref[...], w_ref[...]).astype(BF16)


def _ret_in_proj(hn, cos, sin, w, *, dk_total, tm, tn):
    m, d = hn.shape
    n = w.shape[1]
    hk = dk_total // RET_HEADS
    half = hk // 2
    assert dk_total % tn == 0
    kern = functools.partial(_ret_in_kernel, n_q_blocks=dk_total // tn, n_k_blocks=dk_total // tn,
                             hk=hk, k_scale=float(hk) ** -0.5)
    return pl.pallas_call(
        kern,
        grid=(m // tm, n // tn),
        in_specs=[
            pl.BlockSpec((tm, d), lambda i, j: (i, 0)),
            pl.BlockSpec((tm, half), lambda i, j: (i, 0)),
            pl.BlockSpec((tm, half), lambda i, j: (i, 0)),
            pl.BlockSpec((d, tn), lambda i, j: (0, j)),
        ],
        out_specs=pl.BlockSpec((tm, tn), lambda i, j: (i, j)),
        out_shape=jax.ShapeDtypeStruct((m, n), BF16),
        compiler_params=pltpu.CompilerParams(
            dimension_semantics=("arbitrary", "arbitrary"), vmem_limit_bytes=VMEM_LIMIT_BYTES),
        name="ret_in_proj",
    )(hn, cos, sin, w)


def _ret_rec_kernel(q_ref, k_ref, v_ref, lg_ref, gng_ref, gnb_ref, o_ref,
                    state_ref, dmat_ref, xi_ref, zeta_ref):
    c = CHUNK
    n_heads, dk, dv = state_ref.shape

    @pl.when(pl.program_id(2) == 0)
    def _():
        state_ref[...] = jnp.zeros_like(state_ref)
        row = lax.broadcasted_iota(jnp.int32, (c, c), 0)
        col = lax.broadcasted_iota(jnp.int32, (c, c), 1)
        dpos = (row - col).astype(F32)
        idx_v = lax.broadcasted_iota(jnp.int32, (c, dv), 0).astype(F32)
        idx_k = lax.broadcasted_iota(jnp.int32, (c, dk), 0).astype(F32)
        for hd in range(n_heads):
            dmat_ref[hd] = jnp.where(dpos >= 0, jnp.exp(lg_ref[hd, :, :c] * jnp.maximum(dpos, 0.0)), 0.0)
            xi_ref[hd] = jnp.exp(lg_ref[hd, :, :dv] * (idx_v + 1.0))
            zeta_ref[hd] = jnp.exp(lg_ref[hd, :, :dk] * (c - 1.0 - idx_k))

    def chunk(i, hd):
        rows = pl.ds(i * c, c)
        kcols = slice(hd * dk, (hd + 1) * dk)
        vcols = slice(hd * dv, (hd + 1) * dv)
        q = q_ref[rows, kcols]
        k = k_ref[rows, kcols]
        v = v_ref[rows, vcols]
        s = lax.dot_general(q, k, _NT, preferred_element_type=F32) * dmat_ref[hd]
        st = state_ref[hd]
        o = _dot(s.astype(BF16), v) + _dot(q, st.astype(BF16)) * xi_ref[hd]
        kz = (k.astype(F32) * zeta_ref[hd]).astype(BF16)
        state_ref[hd] = (st * jnp.exp(lg_ref[hd, :, :dv] * float(c))
                         + lax.dot_general(kz, v, _TN, preferred_element_type=F32))

        mu = jnp.mean(o, axis=-1, keepdims=True)
        d = o - mu
        var = jnp.mean(d * d, axis=-1, keepdims=True)
        on = d * lax.rsqrt(var + EPS)
        o_ref[rows, vcols] = (on * gng_ref[:, vcols] + gnb_ref[:, vcols]).astype(BF16)

    for i in range(q_ref.shape[0] // c):
        for hd in range(n_heads):
            chunk(i, hd)


def _ret_recurrence(proj, log_gamma, gn_g, gn_b, *, batch, seq, dk_total, dv_total):
    m = proj.shape[0]
    hps = HEADS_PER_STEP
    hk = dk_total // RET_HEADS
    hv = dv_total // RET_HEADS
    t_rows = CHUNK * RET_CHUNKS_PER_STEP
    nt = seq // t_rows
    width = log_gamma.shape[-1]
    k_off = dk_total // (hps * hk)
    v_off = 2 * dk_total // (hps * hv)
    rows = lambda b, h, t: b * nt + t
    return pl.pallas_call(
        _ret_rec_kernel,
        grid=(batch, RET_HEADS // hps, nt),
        in_specs=[
            pl.BlockSpec((t_rows, hps * hk), lambda b, h, t: (rows(b, h, t), h)),
            pl.BlockSpec((t_rows, hps * hk), lambda b, h, t: (rows(b, h, t), k_off + h)),
            pl.BlockSpec((t_rows, hps * hv), lambda b, h, t: (rows(b, h, t), v_off + h)),
            pl.BlockSpec((hps, 1, width), lambda b, h, t: (h, 0, 0)),
            pl.BlockSpec((1, hps * hv), lambda b, h, t: (0, h)),
            pl.BlockSpec((1, hps * hv), lambda b, h, t: (0, h)),
        ],
        out_specs=pl.BlockSpec((t_rows, hps * hv), lambda b, h, t: (rows(b, h, t), h)),
        out_shape=jax.ShapeDtypeStruct((m, dv_total), BF16),
        scratch_shapes=[
            pltpu.VMEM((hps, hk, hv), F32),
            pltpu.VMEM((hps, CHUNK, CHUNK), F32),
            pltpu.VMEM((hps, CHUNK, hv), F32),
            pltpu.VMEM((hps, CHUNK, hk), F32),
        ],
        compiler_params=pltpu.CompilerParams(
            dimension_semantics=("arbitrary", "arbitrary", "arbitrary"), vmem_limit_bytes=VMEM_LIMIT_BYTES),
        name="ret_recurrence",
    )(proj, proj, proj, log_gamma, gn_g, gn_b)


GLA_IN_PROJ_TILE = (1024, 1024)
RET_IN_PROJ_TILE = (1024, 2048)
OUT_PROJ_TM = 512


def _gla_layer(h, batch, seq, norm_g, w_in, w_a1, w_a2, b_a, head_g, w_out, out_norm_g, pos, inv_freq, final):
    d = h.shape[1]
    dk_total = w_a2.shape[1]
    dv_total = w_out.shape[0]
    rank = w_a1.shape[1]
    wa1 = jnp.zeros((d, GLA_GATE_RANK_PAD), BF16).at[:, :rank].set(w_a1.astype(BF16))
    wa2 = jnp.zeros((GLA_GATE_RANK_PAD, dk_total), BF16).at[:rank, :].set(w_a2.astype(BF16))
    tm, tn = GLA_IN_PROJ_TILE
    proj, la, least = _gla_in_proj(h, norm_g[None, :], w_in.astype(BF16), wa1, wa2, b_a[None, :], tm=tm, tn=tn)
    o = _gla_recurrence(proj, la, least, head_g[None, :], batch=batch, seq=seq, dk_total=dk_total,
                        dv_total=dv_total, least_rows=tm)
    return _out_proj(o, proj, 2 * dk_total + dv_total, w_out.astype(BF16), h, out_norm_g, pos, inv_freq,
                     final=final, tm=OUT_PROJ_TM)


def _ret_layer(h, hn, cos, sin, batch, seq, w_in, gn_g, gn_b, w_out, out_norm_g, pos, inv_freq, final):
    dv_total = w_out.shape[0]
    dk_total = (w_in.shape[1] - 2 * dv_total) // 2
    hk = dk_total // RET_HEADS
    log_gamma = jnp.log1p(-jnp.exp2(-5.0 - jnp.arange(RET_HEADS, dtype=F32)))
    width = max(CHUNK, hk, dv_total // RET_HEADS)
    log_gamma = jnp.broadcast_to(log_gamma[:, None, None], (RET_HEADS, 1, width))
    tm, tn = RET_IN_PROJ_TILE
    proj = _ret_in_proj(hn, cos, sin, w_in.astype(BF16), dk_total=dk_total, tm=tm, tn=tn)
    o = _ret_recurrence(proj, log_gamma, gn_g[None, :], gn_b[None, :],
                        batch=batch, seq=seq, dk_total=dk_total, dv_total=dv_total)
    return _out_proj(o, proj, 2 * dk_total + dv_total, w_out.astype(BF16), h, out_norm_g, pos, inv_freq,
                     final=final, tm=OUT_PROJ_TM)


def kernel(x, positions, gla_norm, gla_w_in, gla_w_a1, gla_w_a2, gla_b_a, gla_head_g, gla_w_out,
           ret_norm, ret_w_in, ret_gn_g, ret_gn_b, ret_w_out, final_norm):
    batch, seq, d = x.shape
    m = batch * seq
    h = x.reshape(m, d)
    pos = positions.reshape(m, 1)
    ret_half = (ret_w_in.shape[2] - 2 * ret_w_out.shape[1]) // (4 * RET_HEADS)
    inv_freq = (ROPE_BASE ** (-jnp.arange(ret_half, dtype=F32) / ret_half))[None, :]
    depth = gla_norm.shape[0] + ret_norm.shape[0]
    assert depth >= 1
    hn = cos = sin = None
    for i in range(depth):
        j = i // 2
        final = i == depth - 1
        if final:
            out_norm_g = final_norm
        else:
            out_norm_g = ret_norm[j] if i % 2 == 0 else gla_norm[j + 1]
        if i % 2 == 0:
            out = _gla_layer(h, batch, seq, gla_norm[j], gla_w_in[j], gla_w_a1[j], gla_w_a2[j], gla_b_a[j],
                             gla_head_g[j], gla_w_out[j], out_norm_g[None, :], pos, inv_freq, final)
        else:
            out = _ret_layer(h, hn, cos, sin, batch, seq, ret_w_in[j], ret_gn_g[j], ret_gn_b[j], ret_w_out[j],
                             out_norm_g[None, :], pos, inv_freq, final)
        if final:
            h = out
        else:
            h, hn, cos, sin = out
    return h.reshape(batch, seq, d)
```

```python
import functools
import math

import jax
import jax.numpy as jnp
from jax import lax
from jax.experimental import pallas as pl
from jax.experimental.pallas import tpu as pltpu

F32 = jnp.float32
BF16 = jnp.bfloat16

EPS = 1e-6
GLA_HEADS = 4
GLA_GATE_RANK_PAD = 128
GLA_GATE_TEMP = 16.0
RET_HEADS = 8
ROPE_BASE = 10000.0

CHUNK = 256
HEADS_PER_STEP = 2
GLA_CHUNKS_PER_STEP = 4
RET_CHUNKS_PER_STEP = 4
FAST_DECAY_LIMIT = 60.0
VMEM_LIMIT_BYTES = 56 * 1024 * 1024
OUT_PROJ_VMEM_LIMIT_BYTES = 60 * 1024 * 1024

_NT = (((1,), (1,)), ((), ()))
_TN = (((0,), (0,)), ((), ()))


def _dot(a, b):
    return jnp.dot(a, b, preferred_element_type=F32)


def _rms_rows(x, g):
    ms = jnp.mean(x * x, axis=-1, keepdims=True)
    return x * lax.rsqrt(ms + EPS) * g


def _silu(g):
    hg = 0.5 * g
    return hg + hg * jnp.tanh(hg)


def _split3(a):
    hi = a.astype(BF16)
    r1 = a - hi.astype(F32)
    mid = r1.astype(BF16)
    lo = (r1 - mid.astype(F32)).astype(BF16)
    return hi, mid, lo


def _causal(c):
    row = lax.broadcasted_iota(jnp.int32, (c, c), 0)
    col = lax.broadcasted_iota(jnp.int32, (c, c), 1)
    return row >= col


def _gla_in_kernel(x_ref, g_ref, w_ref, wa1_ref, wa2_ref, ba_ref, proj_ref, la_ref, least_ref, hn_ref, z_ref):
    j = pl.program_id(1)

    def project():
        proj_ref[...] = _dot(hn_ref[...], w_ref[...]).astype(BF16)

    @pl.when(j == 0)
    def _():
        hn = _rms_rows(x_ref[...], g_ref[...]).astype(BF16)
        hn_ref[...] = hn
        z_ref[...] = _dot(hn, wa1_ref[...]).astype(BF16)
        project()

    @pl.when(j == 1)
    def _():
        xa = _dot(z_ref[...], wa2_ref[...]) + ba_ref[...]
        la = (jnp.minimum(xa, 0.0) - jnp.log(1.0 + jnp.exp(-jnp.abs(xa)))) * (1.0 / GLA_GATE_TEMP)
        la_ref[...] = la
        least = jnp.sum(la[:CHUNK], axis=0, keepdims=True)
        for c0 in range(CHUNK, la.shape[0], CHUNK):
            least = jnp.minimum(least, jnp.sum(la[c0:c0 + CHUNK], axis=0, keepdims=True))
        least_ref[...] = jnp.broadcast_to(least, least_ref.shape)
        project()

    @pl.when(j > 1)
    def _():
        project()


def _gla_in_proj(x2, g, w, wa1, wa2, ba, *, tm, tn):
    m, d = x2.shape
    n = w.shape[1]
    dk = wa2.shape[1]
    assert n // tn >= 2
    assert tm % CHUNK == 0
    return pl.pallas_call(
        _gla_in_kernel,
        grid=(m // tm, n // tn),
        in_specs=[
            pl.BlockSpec((tm, d), lambda i, j: (i, 0)),
            pl.BlockSpec((1, d), lambda i, j: (0, 0)),
            pl.BlockSpec((d, tn), lambda i, j: (0, j)),
            pl.BlockSpec((d, GLA_GATE_RANK_PAD), lambda i, j: (0, 0)),
            pl.BlockSpec((GLA_GATE_RANK_PAD, dk), lambda i, j: (0, 0)),
            pl.BlockSpec((1, dk), lambda i, j: (0, 0)),
        ],
        out_specs=[
            pl.BlockSpec((tm, tn), lambda i, j: (i, j)),
            pl.BlockSpec((tm, dk), lambda i, j: (i, 0)),
            pl.BlockSpec((8, dk), lambda i, j: (i, 0)),
        ],
        out_shape=[
            jax.ShapeDtypeStruct((m, n), BF16),
            jax.ShapeDtypeStruct((m, dk), F32),
            jax.ShapeDtypeStruct((m // tm * 8, dk), F32),
        ],
        scratch_shapes=[pltpu.VMEM((tm, d), BF16), pltpu.VMEM((tm, GLA_GATE_RANK_PAD), BF16)],
        compiler_params=pltpu.CompilerParams(
            dimension_semantics=("arbitrary", "arbitrary"), vmem_limit_bytes=VMEM_LIMIT_BYTES),
        name="gla_in_proj",
    )(x2, g, w, wa1, wa2, ba)


def _gla_rec_kernel(q_ref, k_ref, v_ref, la_ref, least_ref, hg_ref, o_ref,
                    state_ref, ltri_ref, cum_ref, kf_ref, sc_ref, *, q_scale):
    c = CHUNK
    n_heads, dk, dv = state_ref.shape
    n_chunks = q_ref.shape[0] // c

    @pl.when(pl.program_id(2) == 0)
    def _():
        state_ref[...] = jnp.zeros_like(state_ref)
        ltri_ref[...] = jnp.where(_causal(c), 1.0, 0.0).astype(BF16)

    def chunk(i, hd, direct):
        rows = pl.ds(i * c, c)
        kcols = slice(hd * dk, (hd + 1) * dk)
        vcols = slice(hd * dv, (hd + 1) * dv)
        ltri = ltri_ref[...]
        la = la_ref[rows, kcols]
        a_hi, a_mid, a_lo = _split3(la)
        cum = _dot(ltri, a_hi) + _dot(ltri, a_mid) + _dot(ltri, a_lo)
        tot_row = jnp.sum(la, axis=0, keepdims=True)
        tot_col = jnp.broadcast_to(tot_row, (128, dk)).T
        total = cum[c - 1:c, :]

        qf = q_ref[rows, kcols].astype(F32) * q_scale
        kf = k_ref[rows, kcols].astype(F32)
        v = v_ref[rows, vcols]
        qs = (qf * jnp.exp(cum)).astype(BF16)
        ks = (kf * jnp.exp(total - cum)).astype(BF16)

        if direct:
            kn = (kf * jnp.exp(-cum)).astype(BF16)
            s = lax.dot_general(qs, kn, _NT, preferred_element_type=F32)
        else:
            cum_ref[...] = cum
            kf_ref[...] = kf
            sc_ref[...] = jnp.zeros_like(sc_ref)
            lane = lax.broadcasted_iota(jnp.int32, (1, c), 1)

            def column(s, carry):
                d = jnp.exp(jnp.minimum(cum_ref[...] - cum_ref[pl.ds(s, 1), :], 0.0))
                colv = jnp.sum(qf * kf_ref[pl.ds(s, 1), :] * d, axis=1, keepdims=True)
                sc_ref[...] += colv * jnp.where(lane == s, 1.0, 0.0)
                return carry

            lax.fori_loop(0, c, column, 0)
            s = sc_ref[...]
        p = jnp.where(_causal(c), s, 0.0).astype(BF16)

        st = state_ref[hd]
        o = _dot(jnp.concatenate([qs, p], axis=1), jnp.concatenate([st.astype(BF16), v], axis=0))
        decay = jnp.exp(tot_col)
        decay = jnp.concatenate([decay] * (dv // 128), axis=1)
        state_ref[hd] = st * decay + lax.dot_general(ks, v, _TN, preferred_element_type=F32)

        ms = jnp.mean(o * o, axis=-1, keepdims=True)
        on = o * lax.rsqrt(ms + EPS)
        o_ref[rows, vcols] = (on * hg_ref[:, vcols]).astype(BF16)

    def step(direct):
        for i in range(n_chunks):
            for hd in range(n_heads):
                chunk(i, hd, direct)

    direct_ok = jnp.min(least_ref[...]) >= -FAST_DECAY_LIMIT

    @pl.when(direct_ok)
    def _():
        step(True)

    @pl.when(jnp.logical_not(direct_ok))
    def _():
        step(False)


def _gla_recurrence(proj, la, least, head_g, *, batch, seq, dk_total, dv_total, least_rows):
    m = proj.shape[0]
    hps = HEADS_PER_STEP
    hk = dk_total // GLA_HEADS
    hv = dv_total // GLA_HEADS
    t_rows = CHUNK * GLA_CHUNKS_PER_STEP
    nt = seq // t_rows
    k_off = dk_total // (hps * hk)
    v_off = 2 * dk_total // (hps * hv)
    rows = lambda b, h, t: b * nt + t
    assert least_rows % t_rows == 0
    return pl.pallas_call(
        functools.partial(_gla_rec_kernel, q_scale=float(hk) ** -0.5),
        grid=(batch, GLA_HEADS // hps, nt),
        in_specs=[
            pl.BlockSpec((t_rows, hps * hk), lambda b, h, t: (rows(b, h, t), h)),
            pl.BlockSpec((t_rows, hps * hk), lambda b, h, t: (rows(b, h, t), k_off + h)),
            pl.BlockSpec((t_rows, hps * hv), lambda b, h, t: (rows(b, h, t), v_off + h)),
            pl.BlockSpec((t_rows, hps * hk), lambda b, h, t: (rows(b, h, t), h)),
            pl.BlockSpec((8, hps * hk), lambda b, h, t: (rows(b, h, t) // (least_rows // t_rows), h)),
            pl.BlockSpec((1, hps * hv), lambda b, h, t: (0, h)),
        ],
        out_specs=pl.BlockSpec((t_rows, hps * hv), lambda b, h, t: (rows(b, h, t), h)),
        out_shape=jax.ShapeDtypeStruct((m, dv_total), BF16),
        scratch_shapes=[
            pltpu.VMEM((hps, hk, hv), F32),
            pltpu.VMEM((CHUNK, CHUNK), BF16),
            pltpu.VMEM((CHUNK, hk), F32),
            pltpu.VMEM((CHUNK, hk), F32),
            pltpu.VMEM((CHUNK, CHUNK), F32),
        ],
        compiler_params=pltpu.CompilerParams(
            dimension_semantics=("arbitrary", "arbitrary", "arbitrary"), vmem_limit_bytes=VMEM_LIMIT_BYTES),
        name="gla_recurrence",
    )(proj, proj, proj, la, least, head_g)


def _gated_out(a_ref, gate_refs, w_ref, res_ref):
    gw = gate_refs[0].shape[1]
    y = res_ref[...]
    for b, g_ref in enumerate(gate_refs):
        cols = slice(b * gw, (b + 1) * gw)
        gated = (a_ref[:, cols].astype(F32) * _silu(g_ref[...].astype(F32))).astype(BF16)
        y = y + _dot(gated, w_ref[cols, :])
    return y


def _out_proj_final_kernel(a_ref, *refs):
    *gate_refs, w_ref, res_ref, g_ref, o_ref = refs
    o_ref[...] = _rms_rows(_gated_out(a_ref, gate_refs, w_ref, res_ref), g_ref[...])


def _out_proj_next_kernel(a_ref, *refs):
    *gate_refs, w_ref, res_ref, g_ref, pos_ref, invf_ref, o_ref, hn_ref, cos_ref, sin_ref = refs
    y = _gated_out(a_ref, gate_refs, w_ref, res_ref)
    o_ref[...] = y
    hn_ref[...] = _rms_rows(y, g_ref[...]).astype(BF16)
    ang = pos_ref[...].astype(F32) * invf_ref[...]
    cos_ref[...] = jnp.cos(ang)
    sin_ref[...] = jnp.sin(ang)


def _out_proj(a, proj, gate_col, w, res, norm_g, pos=None, inv_freq=None, *, final, tm):
    m, k = a.shape
    n = w.shape[1]
    gw = math.gcd(gate_col, k)
    gate_specs = [pl.BlockSpec((tm, gw), functools.partial(lambda i, c: (i, c), c=gate_col // gw + b))
                  for b in range(k // gw)]
    row_spec = pl.BlockSpec((tm, n), lambda i: (i, 0))
    in_specs = [pl.BlockSpec((tm, k), lambda i: (i, 0))] + gate_specs + [
        pl.BlockSpec((k, n), lambda i: (0, 0), pipeline_mode=pl.Buffered(1)),
        row_spec,
        pl.BlockSpec((1, n), lambda i: (0, 0)),
    ]
    args = (a,) + (proj,) * len(gate_specs) + (w, res, norm_g)
    y_shape = jax.ShapeDtypeStruct((m, n), F32)
    if final:
        out_specs, out_shape = row_spec, y_shape
    else:
        half = inv_freq.shape[1]
        in_specs += [pl.BlockSpec((tm, 1), lambda i: (i, 0)), pl.BlockSpec((1, half), lambda i: (0, 0))]
        args += (pos, inv_freq)
        rot_spec = pl.BlockSpec((tm, half), lambda i: (i, 0))
        rot_shape = jax.ShapeDtypeStruct((m, half), F32)
        out_specs = [row_spec, row_spec, rot_spec, rot_spec]
        out_shape = [y_shape, jax.ShapeDtypeStruct((m, n), BF16), rot_shape, rot_shape]
    return pl.pallas_call(
        _out_proj_final_kernel if final else _out_proj_next_kernel,
        grid=(m // tm,),
        in_specs=in_specs,
        out_specs=out_specs,
        out_shape=out_shape,
        compiler_params=pltpu.CompilerParams(
            dimension_semantics=("arbitrary",), vmem_limit_bytes=OUT_PROJ_VMEM_LIMIT_BYTES),
        name="out_proj_final" if final else "out_proj_next",
    )(*args)


def _ret_in_kernel(hn_ref, cos_ref, sin_ref, w_ref, proj_ref, *, n_q_blocks, n_k_blocks, hk, k_scale):
    j = pl.program_id(1)
    half = hk // 2

    @pl.when(j < n_q_blocks + n_k_blocks)
    def _():
        scale = jnp.where(j >= n_q_blocks, k_scale, 1.0).astype(F32)
        cos = cos_ref[...] * scale
        sin = sin_ref[...] * scale
        acc = _dot(hn_ref[...], w_ref[...])
        for h in range(acc.shape[1] // hk):
            t1 = acc[:, h * hk:h * hk + half]
            t2 = acc[:, h * hk + half:(h + 1) * hk]
            proj_ref[:, h * hk:h * hk + half] = (t1 * cos - t2 * sin).astype(BF16)
            proj_ref[:, h * hk + half:(h + 1) * hk] = (t2 * cos + t1 * sin).astype(BF16)

    @pl.when(j >= n_q_blocks + n_k_blocks)
    def _():
        proj_ref[...] = _dot(hn_ref[...], w_ref[...]).astype(BF16)


def _ret_in_proj(hn, cos, sin, w, *, dk_total, tm, tn):
    m, d = hn.shape
    n = w.shape[1]
    hk = dk_total // RET_HEADS
    half = hk // 2
    assert dk_total % tn == 0
    kern = functools.partial(_ret_in_kernel, n_q_blocks=dk_total // tn, n_k_blocks=dk_total // tn,
                             hk=hk, k_scale=float(hk) ** -0.5)
    return pl.pallas_call(
        kern,
        grid=(m // tm, n // tn),
        in_specs=[
            pl.BlockSpec((tm, d), lambda i, j: (i, 0)),
            pl.BlockSpec((tm, half), lambda i, j: (i, 0)),
            pl.BlockSpec((tm, half), lambda i, j: (i, 0)),
            pl.BlockSpec((d, tn), lambda i, j: (0, j)),
        ],
        out_specs=pl.BlockSpec((tm, tn), lambda i, j: (i, j)),
        out_shape=jax.ShapeDtypeStruct((m, n), BF16),
        compiler_params=pltpu.CompilerParams(
            dimension_semantics=("arbitrary", "arbitrary"), vmem_limit_bytes=VMEM_LIMIT_BYTES),
        name="ret_in_proj",
    )(hn, cos, sin, w)


def _ret_rec_kernel(q_ref, k_ref, v_ref, lg_ref, gng_ref, gnb_ref, o_ref,
                    state_ref, dmat_ref, xi_ref, zeta_ref):
    c = CHUNK
    n_heads, dk, dv = state_ref.shape

    @pl.when(pl.program_id(2) == 0)
    def _():
        state_ref[...] = jnp.zeros_like(state_ref)
        row = lax.broadcasted_iota(jnp.int32, (c, c), 0)
        col = lax.broadcasted_iota(jnp.int32, (c, c), 1)
        dpos = (row - col).astype(F32)
        idx_v = lax.broadcasted_iota(jnp.int32, (c, dv), 0).astype(F32)
        idx_k = lax.broadcasted_iota(jnp.int32, (c, dk), 0).astype(F32)
        for hd in range(n_heads):
            dmat_ref[hd] = jnp.where(dpos >= 0, jnp.exp(lg_ref[hd, :, :c] * jnp.maximum(dpos, 0.0)), 0.0)
            xi_ref[hd] = jnp.exp(lg_ref[hd, :, :dv] * (idx_v + 1.0))
            zeta_ref[hd] = jnp.exp(lg_ref[hd, :, :dk] * (c - 1.0 - idx_k))

    def chunk(i, hd):
        rows = pl.ds(i * c, c)
        kcols = slice(hd * dk, (hd + 1) * dk)
        vcols = slice(hd * dv, (hd + 1) * dv)
        q = q_ref[rows, kcols]
        k = k_ref[rows, kcols]
        v = v_ref[rows, vcols]
        s = lax.dot_general(q, k, _NT, preferred_element_type=F32) * dmat_ref[hd]
        st = state_ref[hd]
        o = _dot(s.astype(BF16), v) + _dot(q, st.astype(BF16)) * xi_ref[hd]
        kz = (k.astype(F32) * zeta_ref[hd]).astype(BF16)
        state_ref[hd] = (st * jnp.exp(lg_ref[hd, :, :dv] * float(c))
                         + lax.dot_general(kz, v, _TN, preferred_element_type=F32))

        mu = jnp.mean(o, axis=-1, keepdims=True)
        d = o - mu
        var = jnp.mean(d * d, axis=-1, keepdims=True)
        on = d * lax.rsqrt(var + EPS)
        o_ref[rows, vcols] = (on * gng_ref[:, vcols] + gnb_ref[:, vcols]).astype(BF16)

    for i in range(q_ref.shape[0] // c):
        for hd in range(n_heads):
            chunk(i, hd)


def _ret_recurrence(proj, log_gamma, gn_g, gn_b, *, batch, seq, dk_total, dv_total):
    m = proj.shape[0]
    hps = HEADS_PER_STEP
    hk = dk_total // RET_HEADS
    hv = dv_total // RET_HEADS
    t_rows = CHUNK * RET_CHUNKS_PER_STEP
    nt = seq // t_rows
    width = log_gamma.shape[-1]
    k_off = dk_total // (hps * hk)
    v_off = 2 * dk_total // (hps * hv)
    rows = lambda b, h, t: b * nt + t
    return pl.pallas_call(
        _ret_rec_kernel,
        grid=(batch, RET_HEADS // hps, nt),
        in_specs=[
            pl.BlockSpec((t_rows, hps * hk), lambda b, h, t: (rows(b, h, t), h)),
            pl.BlockSpec((t_rows, hps * hk), lambda b, h, t: (rows(b, h, t), k_off + h)),
            pl.BlockSpec((t_rows, hps * hv), lambda b, h, t: (rows(b, h, t), v_off + h)),
            pl.BlockSpec((hps, 1, width), lambda b, h, t: (h, 0, 0)),
            pl.BlockSpec((1, hps * hv), lambda b, h, t: (0, h)),
            pl.BlockSpec((1, hps * hv), lambda b, h, t: (0, h)),
        ],
        out_specs=pl.BlockSpec((t_rows, hps * hv), lambda b, h, t: (rows(b, h, t), h)),
        out_shape=jax.ShapeDtypeStruct((m, dv_total), BF16),
        scratch_shapes=[
            pltpu.VMEM((hps, hk, hv), F32),
            pltpu.VMEM((hps, CHUNK, CHUNK), F32),
            pltpu.VMEM((hps, CHUNK, hv), F32),
            pltpu.VMEM((hps, CHUNK, hk), F32),
        ],
        compiler_params=pltpu.CompilerParams(
            dimension_semantics=("arbitrary", "arbitrary", "arbitrary"), vmem_limit_bytes=VMEM_LIMIT_BYTES),
        name="ret_recurrence",
    )(proj, proj, proj, log_gamma, gn_g, gn_b)


GLA_IN_PROJ_TILE = (1024, 1024)
RET_IN_PROJ_TILE = (1024, 2048)
OUT_PROJ_TM = 512


def _gla_layer(h, batch, seq, norm_g, w_in, w_a1, w_a2, b_a, head_g, w_out, out_norm_g, pos, inv_freq, final):
    d = h.shape[1]
    dk_total = w_a2.shape[1]
    dv_total = w_out.shape[0]
    rank = w_a1.shape[1]
    wa1 = jnp.zeros((d, GLA_GATE_RANK_PAD), BF16).at[:, :rank].set(w_a1.astype(BF16))
    wa2 = jnp.zeros((GLA_GATE_RANK_PAD, dk_total), BF16).at[:rank, :].set(w_a2.astype(BF16))
    tm, tn = GLA_IN_PROJ_TILE
    proj, la, least = _gla_in_proj(h, norm_g[None, :], w_in.astype(BF16), wa1, wa2, b_a[None, :], tm=tm, tn=tn)
    o = _gla_recurrence(proj, la, least, head_g[None, :], batch=batch, seq=seq, dk_total=dk_total,
                        dv_total=dv_total, least_rows=tm)
    return _out_proj(o, proj, 2 * dk_total + dv_total, w_out.astype(BF16), h, out_norm_g, pos, inv_freq,
                     final=final, tm=OUT_PROJ_TM)


def _ret_layer(h, hn, cos, sin, batch, seq, w_in, gn_g, gn_b, w_out, out_norm_g, pos, inv_freq, final):
    dv_total = w_out.shape[0]
    dk_total = (w_in.shape[1] - 2 * dv_total) // 2
    hk = dk_total // RET_HEADS
    log_gamma = jnp.log1p(-jnp.exp2(-5.0 - jnp.arange(RET_HEADS, dtype=F32)))
    width = max(CHUNK, hk, dv_total // RET_HEADS)
    log_gamma = jnp.broadcast_to(log_gamma[:, None, None], (RET_HEADS, 1, width))
    tm, tn = RET_IN_PROJ_TILE
    proj = _ret_in_proj(hn, cos, sin, w_in.astype(BF16), dk_total=dk_total, tm=tm, tn=tn)
    o = _ret_recurrence(proj, log_gamma, gn_g[None, :], gn_b[None, :],
                        batch=batch, seq=seq, dk_total=dk_total, dv_total=dv_total)
    return _out_proj(o, proj, 2 * dk_total + dv_total, w_out.astype(BF16), h, out_norm_g, pos, inv_freq,
                     final=final, tm=OUT_PROJ_TM)


def kernel(x, positions, gla_norm, gla_w_in, gla_w_a1, gla_w_a2, gla_b_a, gla_head_g, gla_w_out,
           ret_norm, ret_w_in, ret_gn_g, ret_gn_b, ret_w_out, final_norm):
    batch, seq, d = x.shape
    m = batch * seq
    h = x.reshape(m, d)
    pos = positions.reshape(m, 1)
    ret_half = (ret_w_in.shape[2] - 2 * ret_w_out.shape[1]) // (4 * RET_HEADS)
    inv_freq = (ROPE_BASE ** (-jnp.arange(ret_half, dtype=F32) / ret_half))[None, :]
    depth = gla_norm.shape[0] + ret_norm.shape[0]
    assert depth >= 1
    hn = cos = sin = None
    for i in range(depth):
        j = i // 2
        final = i == depth - 1
        if final:
            out_norm_g = final_norm
        else:
            out_norm_g = ret_norm[j] if i % 2 == 0 else gla_norm[j + 1]
        if i % 2 == 0:
            out = _gla_layer(h, batch, seq, gla_norm[j], gla_w_in[j], gla_w_a1[j], gla_w_a2[j], gla_b_a[j],
                             gla_head_g[j], gla_w_out[j], out_norm_g[None, :], pos, inv_freq, final)
        else:
            out = _ret_layer(h, hn, cos, sin, batch, seq, ret_w_in[j], ret_gn_g[j], ret_gn_b[j], ret_w_out[j],
                             out_norm_g[None, :], pos, inv_freq, final)
        if final:
            h = out
        else:
            h, hn, cos, sin = out
    return h.reshape(batch, seq, d)
```

```python
import functools
import math

import jax
import jax.numpy as jnp
from jax import lax
from jax.experimental import pallas as pl
from jax.experimental.pallas import tpu as pltpu

F32 = jnp.float32
BF16 = jnp.bfloat16

EPS = 1e-6
GLA_HEADS = 4
GLA_GATE_RANK_PAD = 128
GLA_GATE_TEMP = 16.0
RET_HEADS = 8
ROPE_BASE = 10000.0

CHUNK = 256
HEADS_PER_STEP = 2
GLA_CHUNKS_PER_STEP = 4
RET_CHUNKS_PER_STEP = 4
FAST_DECAY_LIMIT = 60.0
VMEM_LIMIT_BYTES = 56 * 1024 * 1024
PROJ_PIECE = 1024
LARGE_VMEM_LIMIT_BYTES = 60 * 1024 * 1024

_NT = (((1,), (1,)), ((), ()))
_TN = (((0,), (0,)), ((), ()))


def _dot(a, b):
    return jnp.dot(a, b, preferred_element_type=F32)


def _rms_rows(x, g):
    ms = jnp.mean(x * x, axis=-1, keepdims=True)
    return x * lax.rsqrt(ms + EPS) * g


def _silu(g):
    hg = 0.5 * g
    return hg + hg * jnp.tanh(hg)


def _split3(a):
    hi = a.astype(BF16)
    r1 = a - hi.astype(F32)
    mid = r1.astype(BF16)
    lo = (r1 - mid.astype(F32)).astype(BF16)
    return hi, mid, lo


def _causal(c):
    row = lax.broadcasted_iota(jnp.int32, (c, c), 0)
    col = lax.broadcasted_iota(jnp.int32, (c, c), 1)
    return row >= col


def _gla_in_kernel(x_ref, g_ref, w_ref, wa1_ref, wa2_ref, ba_ref, proj_ref, la_ref, least_ref, hn_ref, z_ref):
    j = pl.program_id(1)

    def project():
        for c0 in range(0, w_ref.shape[1], PROJ_PIECE):
            cols = slice(c0, c0 + PROJ_PIECE)
            proj_ref[:, cols] = _dot(hn_ref[...], w_ref[:, cols]).astype(BF16)

    @pl.when(j == 0)
    def _():
        hn = _rms_rows(x_ref[...], g_ref[...]).astype(BF16)
        hn_ref[...] = hn
        z_ref[...] = _dot(hn, wa1_ref[...]).astype(BF16)
        project()

    @pl.when(j == 1)
    def _():
        xa = _dot(z_ref[...], wa2_ref[...]) + ba_ref[...]
        la = (jnp.minimum(xa, 0.0) - jnp.log(1.0 + jnp.exp(-jnp.abs(xa)))) * (1.0 / GLA_GATE_TEMP)
        la_ref[...] = la
        least = jnp.sum(la[:CHUNK], axis=0, keepdims=True)
        for c0 in range(CHUNK, la.shape[0], CHUNK):
            least = jnp.minimum(least, jnp.sum(la[c0:c0 + CHUNK], axis=0, keepdims=True))
        least_ref[...] = jnp.broadcast_to(least, least_ref.shape)
        project()

    @pl.when(j > 1)
    def _():
        project()


def _gla_in_proj(x2, g, w, wa1, wa2, ba, *, tm, tn):
    m, d = x2.shape
    n = w.shape[1]
    dk = wa2.shape[1]
    assert n // tn >= 2
    assert tm % CHUNK == 0
    return pl.pallas_call(
        _gla_in_kernel,
        grid=(m // tm, n // tn),
        in_specs=[
            pl.BlockSpec((tm, d), lambda i, j: (i, 0)),
            pl.BlockSpec((1, d), lambda i, j: (0, 0)),
            pl.BlockSpec((d, tn), lambda i, j: (0, j)),
            pl.BlockSpec((d, GLA_GATE_RANK_PAD), lambda i, j: (0, 0)),
            pl.BlockSpec((GLA_GATE_RANK_PAD, dk), lambda i, j: (0, 0)),
            pl.BlockSpec((1, dk), lambda i, j: (0, 0)),
        ],
        out_specs=[
            pl.BlockSpec((tm, tn), lambda i, j: (i, j)),
            pl.BlockSpec((tm, dk), lambda i, j: (i, 0)),
            pl.BlockSpec((8, dk), lambda i, j: (i, 0)),
        ],
        out_shape=[
            jax.ShapeDtypeStruct((m, n), BF16),
            jax.ShapeDtypeStruct((m, dk), F32),
            jax.ShapeDtypeStruct((m // tm * 8, dk), F32),
        ],
        scratch_shapes=[pltpu.VMEM((tm, d), BF16), pltpu.VMEM((tm, GLA_GATE_RANK_PAD), BF16)],
        compiler_params=pltpu.CompilerParams(
            dimension_semantics=("arbitrary", "arbitrary"), vmem_limit_bytes=LARGE_VMEM_LIMIT_BYTES),
        name="gla_in_proj",
    )(x2, g, w, wa1, wa2, ba)


def _gla_rec_kernel(q_ref, k_ref, v_ref, la_ref, least_ref, hg_ref, o_ref,
                    state_ref, ltri_ref, cum_ref, kf_ref, sc_ref, *, q_scale):
    c = CHUNK
    n_heads, dk, dv = state_ref.shape
    n_chunks = q_ref.shape[0] // c

    @pl.when(pl.program_id(2) == 0)
    def _():
        state_ref[...] = jnp.zeros_like(state_ref)
        ltri_ref[...] = jnp.where(_causal(c), 1.0, 0.0).astype(BF16)

    def chunk(i, hd, direct):
        rows = pl.ds(i * c, c)
        kcols = slice(hd * dk, (hd + 1) * dk)
        vcols = slice(hd * dv, (hd + 1) * dv)
        ltri = ltri_ref[...]
        la = la_ref[rows, kcols]
        a_hi, a_mid, a_lo = _split3(la)
        cum = _dot(ltri, a_hi) + _dot(ltri, a_mid) + _dot(ltri, a_lo)
        tot_row = jnp.sum(la, axis=0, keepdims=True)
        tot_col = jnp.broadcast_to(tot_row, (128, dk)).T
        total = cum[c - 1:c, :]

        qf = q_ref[rows, kcols].astype(F32) * q_scale
        kf = k_ref[rows, kcols].astype(F32)
        v = v_ref[rows, vcols]
        qs = (qf * jnp.exp(cum)).astype(BF16)
        ks = (kf * jnp.exp(total - cum)).astype(BF16)

        if direct:
            kn = (kf * jnp.exp(-cum)).astype(BF16)
            s = lax.dot_general(qs, kn, _NT, preferred_element_type=F32)
        else:
            cum_ref[...] = cum
            kf_ref[...] = kf
            sc_ref[...] = jnp.zeros_like(sc_ref)
            lane = lax.broadcasted_iota(jnp.int32, (1, c), 1)

            def column(s, carry):
                d = jnp.exp(jnp.minimum(cum_ref[...] - cum_ref[pl.ds(s, 1), :], 0.0))
                colv = jnp.sum(qf * kf_ref[pl.ds(s, 1), :] * d, axis=1, keepdims=True)
                sc_ref[...] += colv * jnp.where(lane == s, 1.0, 0.0)
                return carry

            lax.fori_loop(0, c, column, 0)
            s = sc_ref[...]
        p = jnp.where(_causal(c), s, 0.0).astype(BF16)

        st = state_ref[hd]
        o = _dot(jnp.concatenate([qs, p], axis=1), jnp.concatenate([st.astype(BF16), v], axis=0))
        decay = jnp.exp(tot_col)
        decay = jnp.concatenate([decay] * (dv // 128), axis=1)
        state_ref[hd] = st * decay + lax.dot_general(ks, v, _TN, preferred_element_type=F32)

        ms = jnp.mean(o * o, axis=-1, keepdims=True)
        on = o * lax.rsqrt(ms + EPS)
        o_ref[rows, vcols] = (on * hg_ref[:, vcols]).astype(BF16)

    def step(direct):
        for i in range(n_chunks):
            for hd in range(n_heads):
                chunk(i, hd, direct)

    direct_ok = jnp.min(least_ref[...]) >= -FAST_DECAY_LIMIT

    @pl.when(direct_ok)
    def _():
        step(True)

    @pl.when(jnp.logical_not(direct_ok))
    def _():
        step(False)


def _gla_recurrence(proj, la, least, head_g, *, batch, seq, dk_total, dv_total, least_rows):
    m = proj.shape[0]
    hps = HEADS_PER_STEP
    hk = dk_total // GLA_HEADS
    hv = dv_total // GLA_HEADS
    t_rows = CHUNK * GLA_CHUNKS_PER_STEP
    nt = seq // t_rows
    k_off = dk_total // (hps * hk)
    v_off = 2 * dk_total // (hps * hv)
    rows = lambda b, h, t: b * nt + t
    assert least_rows % t_rows == 0
    return pl.pallas_call(
        functools.partial(_gla_rec_kernel, q_scale=float(hk) ** -0.5),
        grid=(batch, GLA_HEADS // hps, nt),
        in_specs=[
            pl.BlockSpec((t_rows, hps * hk), lambda b, h, t: (rows(b, h, t), h)),
            pl.BlockSpec((t_rows, hps * hk), lambda b, h, t: (rows(b, h, t), k_off + h)),
            pl.BlockSpec((t_rows, hps * hv), lambda b, h, t: (rows(b, h, t), v_off + h)),
            pl.BlockSpec((t_rows, hps * hk), lambda b, h, t: (rows(b, h, t), h)),
            pl.BlockSpec((8, hps * hk), lambda b, h, t: (rows(b, h, t) // (least_rows // t_rows), h)),
            pl.BlockSpec((1, hps * hv), lambda b, h, t: (0, h)),
        ],
        out_specs=pl.BlockSpec((t_rows, hps * hv), lambda b, h, t: (rows(b, h, t), h)),
        out_shape=jax.ShapeDtypeStruct((m, dv_total), BF16),
        scratch_shapes=[
            pltpu.VMEM((hps, hk, hv), F32),
            pltpu.VMEM((CHUNK, CHUNK), BF16),
            pltpu.VMEM((CHUNK, hk), F32),
            pltpu.VMEM((CHUNK, hk), F32),
            pltpu.VMEM((CHUNK, CHUNK), F32),
        ],
        compiler_params=pltpu.CompilerParams(
            dimension_semantics=("arbitrary", "arbitrary", "arbitrary"), vmem_limit_bytes=VMEM_LIMIT_BYTES),
        name="gla_recurrence",
    )(proj, proj, proj, la, least, head_g)


def _gated_out(a_ref, gate_refs, w_ref, res_ref):
    gw = gate_refs[0].shape[1]
    y = res_ref[...]
    for b, g_ref in enumerate(gate_refs):
        cols = slice(b * gw, (b + 1) * gw)
        gated = (a_ref[:, cols].astype(F32) * _silu(g_ref[...].astype(F32))).astype(BF16)
        y = y + _dot(gated, w_ref[cols, :])
    return y


def _out_proj_final_kernel(a_ref, *refs):
    *gate_refs, w_ref, res_ref, g_ref, o_ref = refs
    o_ref[...] = _rms_rows(_gated_out(a_ref, gate_refs, w_ref, res_ref), g_ref[...])


def _out_proj_next_kernel(a_ref, *refs):
    *gate_refs, w_ref, res_ref, g_ref, pos_ref, invf_ref, o_ref, hn_ref, cos_ref, sin_ref = refs
    y = _gated_out(a_ref, gate_refs, w_ref, res_ref)
    o_ref[...] = y
    hn_ref[...] = _rms_rows(y, g_ref[...]).astype(BF16)
    ang = pos_ref[...].astype(F32) * invf_ref[...]
    cos_ref[...] = jnp.cos(ang)
    sin_ref[...] = jnp.sin(ang)


def _out_proj(a, proj, gate_col, w, res, norm_g, pos=None, inv_freq=None, *, final, tm):
    m, k = a.shape
    n = w.shape[1]
    gw = math.gcd(gate_col, k)
    gate_specs = [pl.BlockSpec((tm, gw), functools.partial(lambda i, c: (i, c), c=gate_col // gw + b))
                  for b in range(k // gw)]
    row_spec = pl.BlockSpec((tm, n), lambda i: (i, 0))
    in_specs = [pl.BlockSpec((tm, k), lambda i: (i, 0))] + gate_specs + [
        pl.BlockSpec((k, n), lambda i: (0, 0), pipeline_mode=pl.Buffered(1)),
        row_spec,
        pl.BlockSpec((1, n), lambda i: (0, 0)),
    ]
    args = (a,) + (proj,) * len(gate_specs) + (w, res, norm_g)
    y_shape = jax.ShapeDtypeStruct((m, n), F32)
    if final:
        out_specs, out_shape = row_spec, y_shape
    else:
        half = inv_freq.shape[1]
        in_specs += [pl.BlockSpec((tm, 1), lambda i: (i, 0)), pl.BlockSpec((1, half), lambda i: (0, 0))]
        args += (pos, inv_freq)
        rot_spec = pl.BlockSpec((tm, half), lambda i: (i, 0))
        rot_shape = jax.ShapeDtypeStruct((m, half), F32)
        out_specs = [row_spec, row_spec, rot_spec, rot_spec]
        out_shape = [y_shape, jax.ShapeDtypeStruct((m, n), BF16), rot_shape, rot_shape]
    return pl.pallas_call(
        _out_proj_final_kernel if final else _out_proj_next_kernel,
        grid=(m // tm,),
        in_specs=in_specs,
        out_specs=out_specs,
        out_shape=out_shape,
        compiler_params=pltpu.CompilerParams(
            dimension_semantics=("arbitrary",), vmem_limit_bytes=LARGE_VMEM_LIMIT_BYTES),
        name="out_proj_final" if final else "out_proj_next",
    )(*args)


def _ret_in_kernel(hn_ref, cos_ref, sin_ref, w_ref, proj_ref, *, n_q_blocks, n_k_blocks, hk, k_scale):
    j = pl.program_id(1)
    half = hk // 2

    @pl.when(j < n_q_blocks + n_k_blocks)
    def _():
        scale = jnp.where(j >= n_q_blocks, k_scale, 1.0).astype(F32)
        cos = cos_ref[...] * scale
        sin = sin_ref[...] * scale
        for c0 in range(0, w_ref.shape[1], PROJ_PIECE):
            acc = _dot(hn_ref[...], w_ref[:, c0:c0 + PROJ_PIECE])
            for h0 in range(0, PROJ_PIECE, hk):
                t1 = acc[:, h0:h0 + half]
                t2 = acc[:, h0 + half:h0 + hk]
                proj_ref[:, c0 + h0:c0 + h0 + half] = (t1 * cos - t2 * sin).astype(BF16)
                proj_ref[:, c0 + h0 + half:c0 + h0 + hk] = (t2 * cos + t1 * sin).astype(BF16)

    @pl.when(j >= n_q_blocks + n_k_blocks)
    def _():
        for c0 in range(0, w_ref.shape[1], PROJ_PIECE):
            cols = slice(c0, c0 + PROJ_PIECE)
            proj_ref[:, cols] = _dot(hn_ref[...], w_ref[:, cols]).astype(BF16)


def _ret_in_proj(hn, cos, sin, w, *, dk_total, tm, tn):
    m, d = hn.shape
    n = w.shape[1]
    hk = dk_total // RET_HEADS
    half = hk // 2
    assert dk_total % tn == 0
    kern = functools.partial(_ret_in_kernel, n_q_blocks=dk_total // tn, n_k_blocks=dk_total // tn,
                             hk=hk, k_scale=float(hk) ** -0.5)
    return pl.pallas_call(
        kern,
        grid=(m // tm, n // tn),
        in_specs=[
            pl.BlockSpec((tm, d), lambda i, j: (i, 0)),
            pl.BlockSpec((tm, half), lambda i, j: (i, 0)),
            pl.BlockSpec((tm, half), lambda i, j: (i, 0)),
            pl.BlockSpec((d, tn), lambda i, j: (0, j)),
        ],
        out_specs=pl.BlockSpec((tm, tn), lambda i, j: (i, j)),
        out_shape=jax.ShapeDtypeStruct((m, n), BF16),
        compiler_params=pltpu.CompilerParams(
            dimension_semantics=("arbitrary", "arbitrary"), vmem_limit_bytes=VMEM_LIMIT_BYTES),
        name="ret_in_proj",
    )(hn, cos, sin, w)


def _ret_rec_kernel(q_ref, k_ref, v_ref, lg_ref, gng_ref, gnb_ref, o_ref,
                    state_ref, dmat_ref, xi_ref, zeta_ref):
    c = CHUNK
    n_heads, dk, dv = state_ref.shape

    @pl.when(pl.program_id(2) == 0)
    def _():
        state_ref[...] = jnp.zeros_like(state_ref)
        row = lax.broadcasted_iota(jnp.int32, (c, c), 0)
        col = lax.broadcasted_iota(jnp.int32, (c, c), 1)
        dpos = (row - col).astype(F32)
        idx_v = lax.broadcasted_iota(jnp.int32, (c, dv), 0).astype(F32)
        idx_k = lax.broadcasted_iota(jnp.int32, (c, dk), 0).astype(F32)
        for hd in range(n_heads):
            dmat_ref[hd] = jnp.where(dpos >= 0, jnp.exp(lg_ref[hd, :, :c] * jnp.maximum(dpos, 0.0)), 0.0)
            xi_ref[hd] = jnp.exp(lg_ref[hd, :, :dv] * (idx_v + 1.0))
            zeta_ref[hd] = jnp.exp(lg_ref[hd, :, :dk] * (c - 1.0 - idx_k))

    def chunk(i, hd):
        rows = pl.ds(i * c, c)
        kcols = slice(hd * dk, (hd + 1) * dk)
        vcols = slice(hd * dv, (hd + 1) * dv)
        q = q_ref[rows, kcols]
        k = k_ref[rows, kcols]
        v = v_ref[rows, vcols]
        s = lax.dot_general(q, k, _NT, preferred_element_type=F32) * dmat_ref[hd]
        st = state_ref[hd]
        o = _dot(s.astype(BF16), v) + _dot(q, st.astype(BF16)) * xi_ref[hd]
        kz = (k.astype(F32) * zeta_ref[hd]).astype(BF16)
        state_ref[hd] = (st * jnp.exp(lg_ref[hd, :, :dv] * float(c))
                         + lax.dot_general(kz, v, _TN, preferred_element_type=F32))

        mu = jnp.mean(o, axis=-1, keepdims=True)
        d = o - mu
        var = jnp.mean(d * d, axis=-1, keepdims=True)
        on = d * lax.rsqrt(var + EPS)
        o_ref[rows, vcols] = (on * gng_ref[:, vcols] + gnb_ref[:, vcols]).astype(BF16)

    for i in range(q_ref.shape[0] // c):
        for hd in range(n_heads):
            chunk(i, hd)


def _ret_recurrence(proj, log_gamma, gn_g, gn_b, *, batch, seq, dk_total, dv_total):
    m = proj.shape[0]
    hps = HEADS_PER_STEP
    hk = dk_total // RET_HEADS
    hv = dv_total // RET_HEADS
    t_rows = CHUNK * RET_CHUNKS_PER_STEP
    nt = seq // t_rows
    width = log_gamma.shape[-1]
    k_off = dk_total // (hps * hk)
    v_off = 2 * dk_total // (hps * hv)
    rows = lambda b, h, t: b * nt + t
    return pl.pallas_call(
        _ret_rec_kernel,
        grid=(batch, RET_HEADS // hps, nt),
        in_specs=[
            pl.BlockSpec((t_rows, hps * hk), lambda b, h, t: (rows(b, h, t), h)),
            pl.BlockSpec((t_rows, hps * hk), lambda b, h, t: (rows(b, h, t), k_off + h)),
            pl.BlockSpec((t_rows, hps * hv), lambda b, h, t: (rows(b, h, t), v_off + h)),
            pl.BlockSpec((hps, 1, width), lambda b, h, t: (h, 0, 0)),
            pl.BlockSpec((1, hps * hv), lambda b, h, t: (0, h)),
            pl.BlockSpec((1, hps * hv), lambda b, h, t: (0, h)),
        ],
        out_specs=pl.BlockSpec((t_rows, hps * hv), lambda b, h, t: (rows(b, h, t), h)),
        out_shape=jax.ShapeDtypeStruct((m, dv_total), BF16),
        scratch_shapes=[
            pltpu.VMEM((hps, hk, hv), F32),
            pltpu.VMEM((hps, CHUNK, CHUNK), F32),
            pltpu.VMEM((hps, CHUNK, hv), F32),
            pltpu.VMEM((hps, CHUNK, hk), F32),
        ],
        compiler_params=pltpu.CompilerParams(
            dimension_semantics=("arbitrary", "arbitrary", "arbitrary"), vmem_limit_bytes=VMEM_LIMIT_BYTES),
        name="ret_recurrence",
    )(proj, proj, proj, log_gamma, gn_g, gn_b)


GLA_IN_PROJ_TILE = (1024, 2048)
RET_IN_PROJ_TILE = (1024, 2048)
OUT_PROJ_TM = 512


def _gla_layer(h, batch, seq, norm_g, w_in, w_a1, w_a2, b_a, head_g, w_out, out_norm_g, pos, inv_freq, final):
    d = h.shape[1]
    dk_total = w_a2.shape[1]
    dv_total = w_out.shape[0]
    rank = w_a1.shape[1]
    wa1 = jnp.zeros((d, GLA_GATE_RANK_PAD), BF16).at[:, :rank].set(w_a1.astype(BF16))
    wa2 = jnp.zeros((GLA_GATE_RANK_PAD, dk_total), BF16).at[:rank, :].set(w_a2.astype(BF16))
    tm, tn = GLA_IN_PROJ_TILE
    proj, la, least = _gla_in_proj(h, norm_g[None, :], w_in.astype(BF16), wa1, wa2, b_a[None, :], tm=tm, tn=tn)
    o = _gla_recurrence(proj, la, least, head_g[None, :], batch=batch, seq=seq, dk_total=dk_total,
                        dv_total=dv_total, least_rows=tm)
    return _out_proj(o, proj, 2 * dk_total + dv_total, w_out.astype(BF16), h, out_norm_g, pos, inv_freq,
                     final=final, tm=OUT_PROJ_TM)


def _ret_layer(h, hn, cos, sin, batch, seq, w_in, gn_g, gn_b, w_out, out_norm_g, pos, inv_freq, final):
    dv_total = w_out.shape[0]
    dk_total = (w_in.shape[1] - 2 * dv_total) // 2
    hk = dk_total // RET_HEADS
    log_gamma = jnp.log1p(-jnp.exp2(-5.0 - jnp.arange(RET_HEADS, dtype=F32)))
    width = max(CHUNK, hk, dv_total // RET_HEADS)
    log_gamma = jnp.broadcast_to(log_gamma[:, None, None], (RET_HEADS, 1, width))
    tm, tn = RET_IN_PROJ_TILE
    proj = _ret_in_proj(hn, cos, sin, w_in.astype(BF16), dk_total=dk_total, tm=tm, tn=tn)
    o = _ret_recurrence(proj, log_gamma, gn_g[None, :], gn_b[None, :],
                        batch=batch, seq=seq, dk_total=dk_total, dv_total=dv_total)
    return _out_proj(o, proj, 2 * dk_total + dv_total, w_out.astype(BF16), h, out_norm_g, pos, inv_freq,
                     final=final, tm=OUT_PROJ_TM)


def kernel(x, positions, gla_norm, gla_w_in, gla_w_a1, gla_w_a2, gla_b_a, gla_head_g, gla_w_out,
           ret_norm, ret_w_in, ret_gn_g, ret_gn_b, ret_w_out, final_norm):
    batch, seq, d = x.shape
    m = batch * seq
    h = x.reshape(m, d)
    pos = positions.reshape(m, 1)
    ret_half = (ret_w_in.shape[2] - 2 * ret_w_out.shape[1]) // (4 * RET_HEADS)
    inv_freq = (ROPE_BASE ** (-jnp.arange(ret_half, dtype=F32) / ret_half))[None, :]
    depth = gla_norm.shape[0] + ret_norm.shape[0]
    assert depth >= 1
    hn = cos = sin = None
    for i in range(depth):
        j = i // 2
        final = i == depth - 1
        if final:
            out_norm_g = final_norm
        else:
            out_norm_g = ret_norm[j] if i % 2 == 0 else gla_norm[j + 1]
        if i % 2 == 0:
            out = _gla_layer(h, batch, seq, gla_norm[j], gla_w_in[j], gla_w_a1[j], gla_w_a2[j], gla_b_a[j],
                             gla_head_g[j], gla_w_out[j], out_norm_g[None, :], pos, inv_freq, final)
        else:
            out = _ret_layer(h, hn, cos, sin, batch, seq, ret_w_in[j], ret_gn_g[j], ret_gn_b[j], ret_w_out[j],
                             out_norm_g[None, :], pos, inv_freq, final)
        if final:
            h = out
        else:
            h, hn, cos, sin = out
    return h.reshape(batch, seq, d)
```

```python
import functools
import math

import jax
import jax.numpy as jnp
from jax import lax
from jax.experimental import pallas as pl
from jax.experimental.pallas import tpu as pltpu

F32 = jnp.float32
BF16 = jnp.bfloat16

EPS = 1e-6
GLA_HEADS = 4
GLA_GATE_RANK_PAD = 128
GLA_GATE_TEMP = 16.0
RET_HEADS = 8
ROPE_BASE = 10000.0

CHUNK = 256
HEADS_PER_STEP = 2
GLA_CHUNKS_PER_STEP = 4
RET_CHUNKS_PER_STEP = 4
FAST_DECAY_LIMIT = 60.0
VMEM_LIMIT_BYTES = 56 * 1024 * 1024
PROJ_PIECE = 1024
LARGE_VMEM_LIMIT_BYTES = 60 * 1024 * 1024

_NT = (((1,), (1,)), ((), ()))
_TN = (((0,), (0,)), ((), ()))


def _dot(a, b):
    return jnp.dot(a, b, preferred_element_type=F32)


def _rms_rows(x, g):
    ms = jnp.mean(x * x, axis=-1, keepdims=True)
    return x * lax.rsqrt(ms + EPS) * g


def _silu(g):
    hg = 0.5 * g
    return hg + hg * jnp.tanh(hg)


def _split3(a):
    hi = a.astype(BF16)
    r1 = a - hi.astype(F32)
    mid = r1.astype(BF16)
    lo = (r1 - mid.astype(F32)).astype(BF16)
    return hi, mid, lo


def _causal(c):
    row = lax.broadcasted_iota(jnp.int32, (c, c), 0)
    col = lax.broadcasted_iota(jnp.int32, (c, c), 1)
    return row >= col


def _staggered(bodies):
    started = []
    pending = list(bodies)
    while pending or started:
        if pending:
            started.append(pending.pop(0))
        for body in list(started):
            try:
                next(body)
            except StopIteration:
                started.remove(body)


def _gla_in_kernel(x_ref, g_ref, w_ref, wa1_ref, wa2_ref, ba_ref, proj_ref, la_ref, least_ref, hn_ref, z_ref):
    j = pl.program_id(1)

    def project():
        for c0 in range(0, w_ref.shape[1], PROJ_PIECE):
            cols = slice(c0, c0 + PROJ_PIECE)
            proj_ref[:, cols] = _dot(hn_ref[...], w_ref[:, cols]).astype(BF16)

    @pl.when(j == 0)
    def _():
        hn = _rms_rows(x_ref[...], g_ref[...]).astype(BF16)
        hn_ref[...] = hn
        z_ref[...] = _dot(hn, wa1_ref[...]).astype(BF16)
        project()

    @pl.when(j == 1)
    def _():
        xa = _dot(z_ref[...], wa2_ref[...]) + ba_ref[...]
        la = (jnp.minimum(xa, 0.0) - jnp.log(1.0 + jnp.exp(-jnp.abs(xa)))) * (1.0 / GLA_GATE_TEMP)
        la_ref[...] = la
        least = jnp.sum(la[:CHUNK], axis=0, keepdims=True)
        for c0 in range(CHUNK, la.shape[0], CHUNK):
            least = jnp.minimum(least, jnp.sum(la[c0:c0 + CHUNK], axis=0, keepdims=True))
        least_ref[...] = jnp.broadcast_to(least, least_ref.shape)
        project()

    @pl.when(j > 1)
    def _():
        project()


def _gla_in_proj(x2, g, w, wa1, wa2, ba, *, tm, tn):
    m, d = x2.shape
    n = w.shape[1]
    dk = wa2.shape[1]
    assert n // tn >= 2
    assert tm % CHUNK == 0
    return pl.pallas_call(
        _gla_in_kernel,
        grid=(m // tm, n // tn),
        in_specs=[
            pl.BlockSpec((tm, d), lambda i, j: (i, 0)),
            pl.BlockSpec((1, d), lambda i, j: (0, 0)),
            pl.BlockSpec((d, tn), lambda i, j: (0, j)),
            pl.BlockSpec((d, GLA_GATE_RANK_PAD), lambda i, j: (0, 0)),
            pl.BlockSpec((GLA_GATE_RANK_PAD, dk), lambda i, j: (0, 0)),
            pl.BlockSpec((1, dk), lambda i, j: (0, 0)),
        ],
        out_specs=[
            pl.BlockSpec((tm, tn), lambda i, j: (i, j)),
            pl.BlockSpec((tm, dk), lambda i, j: (i, 0)),
            pl.BlockSpec((8, dk), lambda i, j: (i, 0)),
        ],
        out_shape=[
            jax.ShapeDtypeStruct((m, n), BF16),
            jax.ShapeDtypeStruct((m, dk), F32),
            jax.ShapeDtypeStruct((m // tm * 8, dk), F32),
        ],
        scratch_shapes=[pltpu.VMEM((tm, d), BF16), pltpu.VMEM((tm, GLA_GATE_RANK_PAD), BF16)],
        compiler_params=pltpu.CompilerParams(
            dimension_semantics=("arbitrary", "arbitrary"), vmem_limit_bytes=LARGE_VMEM_LIMIT_BYTES),
        name="gla_in_proj",
    )(x2, g, w, wa1, wa2, ba)


def _gla_rec_kernel(q_ref, k_ref, v_ref, la_ref, least_ref, hg_ref, o_ref,
                    state_ref, ltri_ref, cum_ref, kf_ref, sc_ref, *, q_scale):
    c = CHUNK
    n_heads, dk, dv = state_ref.shape
    n_chunks = q_ref.shape[0] // c

    @pl.when(pl.program_id(2) == 0)
    def _():
        state_ref[...] = jnp.zeros_like(state_ref)
        ltri_ref[...] = jnp.where(_causal(c), 1.0, 0.0).astype(BF16)

    def chunk(i, hd, direct):
        rows = pl.ds(i * c, c)
        kcols = slice(hd * dk, (hd + 1) * dk)
        vcols = slice(hd * dv, (hd + 1) * dv)
        ltri = ltri_ref[...]
        la = la_ref[rows, kcols]
        a_hi, a_mid, a_lo = _split3(la)
        cum = _dot(ltri, a_hi) + _dot(ltri, a_mid) + _dot(ltri, a_lo)
        tot_row = jnp.sum(la, axis=0, keepdims=True)
        tot_col = jnp.broadcast_to(tot_row, (128, dk)).T
        total = cum[c - 1:c, :]
        yield

        qf = q_ref[rows, kcols].astype(F32) * q_scale
        kf = k_ref[rows, kcols].astype(F32)
        v = v_ref[rows, vcols]
        qs = (qf * jnp.exp(cum)).astype(BF16)
        ks = (kf * jnp.exp(total - cum)).astype(BF16)

        if direct:
            kn = (kf * jnp.exp(-cum)).astype(BF16)
            s = lax.dot_general(qs, kn, _NT, preferred_element_type=F32)
        else:
            cum_ref[...] = cum
            kf_ref[...] = kf
            sc_ref[...] = jnp.zeros_like(sc_ref)
            lane = lax.broadcasted_iota(jnp.int32, (1, c), 1)

            def column(s, carry):
                d = jnp.exp(jnp.minimum(cum_ref[...] - cum_ref[pl.ds(s, 1), :], 0.0))
                colv = jnp.sum(qf * kf_ref[pl.ds(s, 1), :] * d, axis=1, keepdims=True)
                sc_ref[...] += colv * jnp.where(lane == s, 1.0, 0.0)
                return carry

            lax.fori_loop(0, c, column, 0)
            s = sc_ref[...]
        yield
        p = jnp.where(_causal(c), s, 0.0).astype(BF16)

        st = state_ref[hd]
        o = _dot(jnp.concatenate([qs, p], axis=1), jnp.concatenate([st.astype(BF16), v], axis=0))
        yield
        decay = jnp.exp(tot_col)
        decay = jnp.concatenate([decay] * (dv // 128), axis=1)
        state_ref[hd] = st * decay + lax.dot_general(ks, v, _TN, preferred_element_type=F32)

        yield
        ms = jnp.mean(o * o, axis=-1, keepdims=True)
        on = o * lax.rsqrt(ms + EPS)
        o_ref[rows, vcols] = (on * hg_ref[:, vcols]).astype(BF16)

    def step(direct):
        _staggered([chunk(i, hd, direct) for i in range(n_chunks) for hd in range(n_heads)])

    direct_ok = jnp.min(least_ref[...]) >= -FAST_DECAY_LIMIT

    @pl.when(direct_ok)
    def _():
        step(True)

    @pl.when(jnp.logical_not(direct_ok))
    def _():
        step(False)


def _gla_recurrence(proj, la, least, head_g, *, batch, seq, dk_total, dv_total, least_rows):
    m = proj.shape[0]
    hps = HEADS_PER_STEP
    hk = dk_total // GLA_HEADS
    hv = dv_total // GLA_HEADS
    t_rows = CHUNK * GLA_CHUNKS_PER_STEP
    nt = seq // t_rows
    k_off = dk_total // (hps * hk)
    v_off = 2 * dk_total // (hps * hv)
    rows = lambda b, h, t: b * nt + t
    assert least_rows % t_rows == 0
    return pl.pallas_call(
        functools.partial(_gla_rec_kernel, q_scale=float(hk) ** -0.5),
        grid=(batch, GLA_HEADS // hps, nt),
        in_specs=[
            pl.BlockSpec((t_rows, hps * hk), lambda b, h, t: (rows(b, h, t), h)),
            pl.BlockSpec((t_rows, hps * hk), lambda b, h, t: (rows(b, h, t), k_off + h)),
            pl.BlockSpec((t_rows, hps * hv), lambda b, h, t: (rows(b, h, t), v_off + h)),
            pl.BlockSpec((t_rows, hps * hk), lambda b, h, t: (rows(b, h, t), h)),
            pl.BlockSpec((8, hps * hk), lambda b, h, t: (rows(b, h, t) // (least_rows // t_rows), h)),
            pl.BlockSpec((1, hps * hv), lambda b, h, t: (0, h)),
        ],
        out_specs=pl.BlockSpec((t_rows, hps * hv), lambda b, h, t: (rows(b, h, t), h)),
        out_shape=jax.ShapeDtypeStruct((m, dv_total), BF16),
        scratch_shapes=[
            pltpu.VMEM((hps, hk, hv), F32),
            pltpu.VMEM((CHUNK, CHUNK), BF16),
            pltpu.VMEM((CHUNK, hk), F32),
            pltpu.VMEM((CHUNK, hk), F32),
            pltpu.VMEM((CHUNK, CHUNK), F32),
        ],
        compiler_params=pltpu.CompilerParams(
            dimension_semantics=("arbitrary", "arbitrary", "arbitrary"), vmem_limit_bytes=VMEM_LIMIT_BYTES),
        name="gla_recurrence",
    )(proj, proj, proj, la, least, head_g)


def _gated_out(a_ref, gate_refs, w_ref, res_ref):
    gw = gate_refs[0].shape[1]
    y = res_ref[...]
    for b, g_ref in enumerate(gate_refs):
        cols = slice(b * gw, (b + 1) * gw)
        gated = (a_ref[:, cols].astype(F32) * _silu(g_ref[...].astype(F32))).astype(BF16)
        y = y + _dot(gated, w_ref[cols, :])
    return y


def _out_proj_final_kernel(a_ref, *refs):
    *gate_refs, w_ref, res_ref, g_ref, o_ref = refs
    o_ref[...] = _rms_rows(_gated_out(a_ref, gate_refs, w_ref, res_ref), g_ref[...])


def _out_proj_next_kernel(a_ref, *refs):
    *gate_refs, w_ref, res_ref, g_ref, pos_ref, invf_ref, o_ref, hn_ref, cos_ref, sin_ref = refs
    y = _gated_out(a_ref, gate_refs, w_ref, res_ref)
    o_ref[...] = y
    hn_ref[...] = _rms_rows(y, g_ref[...]).astype(BF16)
    ang = pos_ref[...].astype(F32) * invf_ref[...]
    cos_ref[...] = jnp.cos(ang)
    sin_ref[...] = jnp.sin(ang)


def _out_proj(a, proj, gate_col, w, res, norm_g, pos=None, inv_freq=None, *, final, tm):
    m, k = a.shape
    n = w.shape[1]
    gw = math.gcd(gate_col, k)
    gate_specs = [pl.BlockSpec((tm, gw), functools.partial(lambda i, c: (i, c), c=gate_col // gw + b))
                  for b in range(k // gw)]
    row_spec = pl.BlockSpec((tm, n), lambda i: (i, 0))
    in_specs = [pl.BlockSpec((tm, k), lambda i: (i, 0))] + gate_specs + [
        pl.BlockSpec((k, n), lambda i: (0, 0), pipeline_mode=pl.Buffered(1)),
        row_spec,
        pl.BlockSpec((1, n), lambda i: (0, 0)),
    ]
    args = (a,) + (proj,) * len(gate_specs) + (w, res, norm_g)
    y_shape = jax.ShapeDtypeStruct((m, n), F32)
    if final:
        out_specs, out_shape = row_spec, y_shape
    else:
        half = inv_freq.shape[1]
        in_specs += [pl.BlockSpec((tm, 1), lambda i: (i, 0)), pl.BlockSpec((1, half), lambda i: (0, 0))]
        args += (pos, inv_freq)
        rot_spec = pl.BlockSpec((tm, half), lambda i: (i, 0))
        rot_shape = jax.ShapeDtypeStruct((m, half), F32)
        out_specs = [row_spec, row_spec, rot_spec, rot_spec]
        out_shape = [y_shape, jax.ShapeDtypeStruct((m, n), BF16), rot_shape, rot_shape]
    return pl.pallas_call(
        _out_proj_final_kernel if final else _out_proj_next_kernel,
        grid=(m // tm,),
        in_specs=in_specs,
        out_specs=out_specs,
        out_shape=out_shape,
        compiler_params=pltpu.CompilerParams(
            dimension_semantics=("arbitrary",), vmem_limit_bytes=LARGE_VMEM_LIMIT_BYTES),
        name="out_proj_final" if final else "out_proj_next",
    )(*args)


def _ret_in_kernel(hn_ref, cos_ref, sin_ref, w_ref, proj_ref, *, n_q_blocks, n_k_blocks, hk, k_scale):
    j = pl.program_id(1)
    half = hk // 2

    @pl.when(j < n_q_blocks + n_k_blocks)
    def _():
        scale = jnp.where(j >= n_q_blocks, k_scale, 1.0).astype(F32)
        cos = cos_ref[...] * scale
        sin = sin_ref[...] * scale
        for c0 in range(0, w_ref.shape[1], PROJ_PIECE):
            acc = _dot(hn_ref[...], w_ref[:, c0:c0 + PROJ_PIECE])
            for h0 in range(0, PROJ_PIECE, hk):
                t1 = acc[:, h0:h0 + half]
                t2 = acc[:, h0 + half:h0 + hk]
                proj_ref[:, c0 + h0:c0 + h0 + half] = (t1 * cos - t2 * sin).astype(BF16)
                proj_ref[:, c0 + h0 + half:c0 + h0 + hk] = (t2 * cos + t1 * sin).astype(BF16)

    @pl.when(j >= n_q_blocks + n_k_blocks)
    def _():
        for c0 in range(0, w_ref.shape[1], PROJ_PIECE):
            cols = slice(c0, c0 + PROJ_PIECE)
            proj_ref[:, cols] = _dot(hn_ref[...], w_ref[:, cols]).astype(BF16)


def _ret_in_proj(hn, cos, sin, w, *, dk_total, tm, tn):
    m, d = hn.shape
    n = w.shape[1]
    hk = dk_total // RET_HEADS
    half = hk // 2
    assert dk_total % tn == 0
    kern = functools.partial(_ret_in_kernel, n_q_blocks=dk_total // tn, n_k_blocks=dk_total // tn,
                             hk=hk, k_scale=float(hk) ** -0.5)
    return pl.pallas_call(
        kern,
        grid=(m // tm, n // tn),
        in_specs=[
            pl.BlockSpec((tm, d), lambda i, j: (i, 0)),
            pl.BlockSpec((tm, half), lambda i, j: (i, 0)),
            pl.BlockSpec((tm, half), lambda i, j: (i, 0)),
            pl.BlockSpec((d, tn), lambda i, j: (0, j)),
        ],
        out_specs=pl.BlockSpec((tm, tn), lambda i, j: (i, j)),
        out_shape=jax.ShapeDtypeStruct((m, n), BF16),
        compiler_params=pltpu.CompilerParams(
            dimension_semantics=("arbitrary", "arbitrary"), vmem_limit_bytes=VMEM_LIMIT_BYTES),
        name="ret_in_proj",
    )(hn, cos, sin, w)


def _ret_rec_kernel(q_ref, k_ref, v_ref, lg_ref, gng_ref, gnb_ref, o_ref,
                    state_ref, dmat_ref, xi_ref, zeta_ref):
    c = CHUNK
    n_heads, dk, dv = state_ref.shape

    @pl.when(pl.program_id(2) == 0)
    def _():
        state_ref[...] = jnp.zeros_like(state_ref)
        row = lax.broadcasted_iota(jnp.int32, (c, c), 0)
        col = lax.broadcasted_iota(jnp.int32, (c, c), 1)
        dpos = (row - col).astype(F32)
        idx_v = lax.broadcasted_iota(jnp.int32, (c, dv), 0).astype(F32)
        idx_k = lax.broadcasted_iota(jnp.int32, (c, dk), 0).astype(F32)
        for hd in range(n_heads):
            dmat_ref[hd] = jnp.where(dpos >= 0, jnp.exp(lg_ref[hd, :, :c] * jnp.maximum(dpos, 0.0)), 0.0)
            xi_ref[hd] = jnp.exp(lg_ref[hd, :, :dv] * (idx_v + 1.0))
            zeta_ref[hd] = jnp.exp(lg_ref[hd, :, :dk] * (c - 1.0 - idx_k))

    def chunk(i, hd):
        rows = pl.ds(i * c, c)
        kcols = slice(hd * dk, (hd + 1) * dk)
        vcols = slice(hd * dv, (hd + 1) * dv)
        q = q_ref[rows, kcols]
        k = k_ref[rows, kcols]
        v = v_ref[rows, vcols]
        s = lax.dot_general(q, k, _NT, preferred_element_type=F32) * dmat_ref[hd]
        st = state_ref[hd]
        o = _dot(s.astype(BF16), v) + _dot(q, st.astype(BF16)) * xi_ref[hd]
        kz = (k.astype(F32) * zeta_ref[hd]).astype(BF16)
        state_ref[hd] = (st * jnp.exp(lg_ref[hd, :, :dv] * float(c))
                         + lax.dot_general(kz, v, _TN, preferred_element_type=F32))

        mu = jnp.mean(o, axis=-1, keepdims=True)
        d = o - mu
        var = jnp.mean(d * d, axis=-1, keepdims=True)
        on = d * lax.rsqrt(var + EPS)
        o_ref[rows, vcols] = (on * gng_ref[:, vcols] + gnb_ref[:, vcols]).astype(BF16)

    for i in range(q_ref.shape[0] // c):
        for hd in range(n_heads):
            chunk(i, hd)


def _ret_recurrence(proj, log_gamma, gn_g, gn_b, *, batch, seq, dk_total, dv_total):
    m = proj.shape[0]
    hps = HEADS_PER_STEP
    hk = dk_total // RET_HEADS
    hv = dv_total // RET_HEADS
    t_rows = CHUNK * RET_CHUNKS_PER_STEP
    nt = seq // t_rows
    width = log_gamma.shape[-1]
    k_off = dk_total // (hps * hk)
    v_off = 2 * dk_total // (hps * hv)
    rows = lambda b, h, t: b * nt + t
    return pl.pallas_call(
        _ret_rec_kernel,
        grid=(batch, RET_HEADS // hps, nt),
        in_specs=[
            pl.BlockSpec((t_rows, hps * hk), lambda b, h, t: (rows(b, h, t), h)),
            pl.BlockSpec((t_rows, hps * hk), lambda b, h, t: (rows(b, h, t), k_off + h)),
            pl.BlockSpec((t_rows, hps * hv), lambda b, h, t: (rows(b, h, t), v_off + h)),
            pl.BlockSpec((hps, 1, width), lambda b, h, t: (h, 0, 0)),
            pl.BlockSpec((1, hps * hv), lambda b, h, t: (0, h)),
            pl.BlockSpec((1, hps * hv), lambda b, h, t: (0, h)),
        ],
        out_specs=pl.BlockSpec((t_rows, hps * hv), lambda b, h, t: (rows(b, h, t), h)),
        out_shape=jax.ShapeDtypeStruct((m, dv_total), BF16),
        scratch_shapes=[
            pltpu.VMEM((hps, hk, hv), F32),
            pltpu.VMEM((hps, CHUNK, CHUNK), F32),
            pltpu.VMEM((hps, CHUNK, hv), F32),
            pltpu.VMEM((hps, CHUNK, hk), F32),
        ],
        compiler_params=pltpu.CompilerParams(
            dimension_semantics=("arbitrary", "arbitrary", "arbitrary"), vmem_limit_bytes=VMEM_LIMIT_BYTES),
        name="ret_recurrence",
    )(proj, proj, proj, log_gamma, gn_g, gn_b)


GLA_IN_PROJ_TILE = (1024, 2048)
RET_IN_PROJ_TILE = (1024, 2048)
OUT_PROJ_TM = 512


def _gla_layer(h, batch, seq, norm_g, w_in, w_a1, w_a2, b_a, head_g, w_out, out_norm_g, pos, inv_freq, final):
    d = h.shape[1]
    dk_total = w_a2.shape[1]
    dv_total = w_out.shape[0]
    rank = w_a1.shape[1]
    wa1 = jnp.zeros((d, GLA_GATE_RANK_PAD), BF16).at[:, :rank].set(w_a1.astype(BF16))
    wa2 = jnp.zeros((GLA_GATE_RANK_PAD, dk_total), BF16).at[:rank, :].set(w_a2.astype(BF16))
    tm, tn = GLA_IN_PROJ_TILE
    proj, la, least = _gla_in_proj(h, norm_g[None, :], w_in.astype(BF16), wa1, wa2, b_a[None, :], tm=tm, tn=tn)
    o = _gla_recurrence(proj, la, least, head_g[None, :], batch=batch, seq=seq, dk_total=dk_total,
                        dv_total=dv_total, least_rows=tm)
    return _out_proj(o, proj, 2 * dk_total + dv_total, w_out.astype(BF16), h, out_norm_g, pos, inv_freq,
                     final=final, tm=OUT_PROJ_TM)


def _ret_layer(h, hn, cos, sin, batch, seq, w_in, gn_g, gn_b, w_out, out_norm_g, pos, inv_freq, final):
    dv_total = w_out.shape[0]
    dk_total = (w_in.shape[1] - 2 * dv_total) // 2
    hk = dk_total // RET_HEADS
    log_gamma = jnp.log1p(-jnp.exp2(-5.0 - jnp.arange(RET_HEADS, dtype=F32)))
    width = max(CHUNK, hk, dv_total // RET_HEADS)
    log_gamma = jnp.broadcast_to(log_gamma[:, None, None], (RET_HEADS, 1, width))
    tm, tn = RET_IN_PROJ_TILE
    proj = _ret_in_proj(hn, cos, sin, w_in.astype(BF16), dk_total=dk_total, tm=tm, tn=tn)
    o = _ret_recurrence(proj, log_gamma, gn_g[None, :], gn_b[None, :],
                        batch=batch, seq=seq, dk_total=dk_total, dv_total=dv_total)
    return _out_proj(o, proj, 2 * dk_total + dv_total, w_out.astype(BF16), h, out_norm_g, pos, inv_freq,
                     final=final, tm=OUT_PROJ_TM)


def kernel(x, positions, gla_norm, gla_w_in, gla_w_a1, gla_w_a2, gla_b_a, gla_head_g, gla_w_out,
           ret_norm, ret_w_in, ret_gn_g, ret_gn_b, ret_w_out, final_norm):
    batch, seq, d = x.shape
    m = batch * seq
    h = x.reshape(m, d)
    pos = positions.reshape(m, 1)
    ret_half = (ret_w_in.shape[2] - 2 * ret_w_out.shape[1]) // (4 * RET_HEADS)
    inv_freq = (ROPE_BASE ** (-jnp.arange(ret_half, dtype=F32) / ret_half))[None, :]
    depth = gla_norm.shape[0] + ret_norm.shape[0]
    assert depth >= 1
    hn = cos = sin = None
    for i in range(depth):
        j = i // 2
        final = i == depth - 1
        if final:
            out_norm_g = final_norm
        else:
            out_norm_g = ret_norm[j] if i % 2 == 0 else gla_norm[j + 1]
        if i % 2 == 0:
            out = _gla_layer(h, batch, seq, gla_norm[j], gla_w_in[j], gla_w_a1[j], gla_w_a2[j], gla_b_a[j],
                             gla_head_g[j], gla_w_out[j], out_norm_g[None, :], pos, inv_freq, final)
        else:
            out = _ret_layer(h, hn, cos, sin, batch, seq, ret_w_in[j], ret_gn_g[j], ret_gn_b[j], ret_w_out[j],
                             out_norm_g[None, :], pos, inv_freq, final)
        if final:
            h = out
        else:
            h, hn, cos, sin = out
    return h.reshape(batch, seq, d)
```

```python
import functools
import math

import jax
import jax.numpy as jnp
from jax import lax
from jax.experimental import pallas as pl
from jax.experimental.pallas import tpu as pltpu

F32 = jnp.float32
BF16 = jnp.bfloat16

EPS = 1e-6
GLA_HEADS = 4
GLA_GATE_RANK_PAD = 128
GLA_GATE_TEMP = 16.0
RET_HEADS = 8
ROPE_BASE = 10000.0

CHUNK = 256
HEADS_PER_STEP = 2
GLA_CHUNKS_PER_STEP = 4
RET_CHUNKS_PER_STEP = 4
FAST_DECAY_LIMIT = 60.0
VMEM_LIMIT_BYTES = 56 * 1024 * 1024
PROJ_PIECE = 1024
LARGE_VMEM_LIMIT_BYTES = 60 * 1024 * 1024

_NT = (((1,), (1,)), ((), ()))
_TN = (((0,), (0,)), ((), ()))


def _dot(a, b):
    return jnp.dot(a, b, preferred_element_type=F32)


def _rms_rows(x, g):
    ms = jnp.mean(x * x, axis=-1, keepdims=True)
    return x * lax.rsqrt(ms + EPS) * g


def _silu(g):
    hg = 0.5 * g
    return hg + hg * jnp.tanh(hg)


def _split3(a):
    hi = a.astype(BF16)
    r1 = a - hi.astype(F32)
    mid = r1.astype(BF16)
    lo = (r1 - mid.astype(F32)).astype(BF16)
    return hi, mid, lo


def _causal(c):
    row = lax.broadcasted_iota(jnp.int32, (c, c), 0)
    col = lax.broadcasted_iota(jnp.int32, (c, c), 1)
    return row >= col


def _staggered(bodies):
    started = []
    pending = list(bodies)
    while pending or started:
        if pending:
            started.append(pending.pop(0))
        for body in list(reversed(started)):
            try:
                next(body)
            except StopIteration:
                started.remove(body)


def _gla_in_kernel(x_ref, g_ref, w_ref, wa1_ref, wa2_ref, ba_ref, proj_ref, la_ref, least_ref, hn_ref, z_ref):
    j = pl.program_id(1)

    def project():
        for c0 in range(0, w_ref.shape[1], PROJ_PIECE):
            cols = slice(c0, c0 + PROJ_PIECE)
            proj_ref[:, cols] = _dot(hn_ref[...], w_ref[:, cols]).astype(BF16)

    @pl.when(j == 0)
    def _():
        hn = _rms_rows(x_ref[...], g_ref[...]).astype(BF16)
        hn_ref[...] = hn
        z_ref[...] = _dot(hn, wa1_ref[...]).astype(BF16)
        project()

    @pl.when(j == 1)
    def _():
        xa = _dot(z_ref[...], wa2_ref[...]) + ba_ref[...]
        la = (jnp.minimum(xa, 0.0) - jnp.log(1.0 + jnp.exp(-jnp.abs(xa)))) * (1.0 / GLA_GATE_TEMP)
        la_ref[...] = la
        least = jnp.sum(la[:CHUNK], axis=0, keepdims=True)
        for c0 in range(CHUNK, la.shape[0], CHUNK):
            least = jnp.minimum(least, jnp.sum(la[c0:c0 + CHUNK], axis=0, keepdims=True))
        least_ref[...] = jnp.broadcast_to(least, least_ref.shape)
        project()

    @pl.when(j > 1)
    def _():
        project()


def _gla_in_proj(x2, g, w, wa1, wa2, ba, *, tm, tn):
    m, d = x2.shape
    n = w.shape[1]
    dk = wa2.shape[1]
    assert n // tn >= 2
    assert tm % CHUNK == 0
    return pl.pallas_call(
        _gla_in_kernel,
        grid=(m // tm, n // tn),
        in_specs=[
            pl.BlockSpec((tm, d), lambda i, j: (i, 0)),
            pl.BlockSpec((1, d), lambda i, j: (0, 0)),
            pl.BlockSpec((d, tn), lambda i, j: (0, j)),
            pl.BlockSpec((d, GLA_GATE_RANK_PAD), lambda i, j: (0, 0)),
            pl.BlockSpec((GLA_GATE_RANK_PAD, dk), lambda i, j: (0, 0)),
            pl.BlockSpec((1, dk), lambda i, j: (0, 0)),
        ],
        out_specs=[
            pl.BlockSpec((tm, tn), lambda i, j: (i, j)),
            pl.BlockSpec((tm, dk), lambda i, j: (i, 0)),
            pl.BlockSpec((8, dk), lambda i, j: (i, 0)),
        ],
        out_shape=[
            jax.ShapeDtypeStruct((m, n), BF16),
            jax.ShapeDtypeStruct((m, dk), F32),
            jax.ShapeDtypeStruct((m // tm * 8, dk), F32),
        ],
        scratch_shapes=[pltpu.VMEM((tm, d), BF16), pltpu.VMEM((tm, GLA_GATE_RANK_PAD), BF16)],
        compiler_params=pltpu.CompilerParams(
            dimension_semantics=("arbitrary", "arbitrary"), vmem_limit_bytes=LARGE_VMEM_LIMIT_BYTES),
        name="gla_in_proj",
    )(x2, g, w, wa1, wa2, ba)


def _gla_rec_kernel(q_ref, k_ref, v_ref, la_ref, least_ref, hg_ref, o_ref,
                    state_ref, ltri_ref, cum_ref, kf_ref, sc_ref, *, q_scale):
    c = CHUNK
    n_heads, dk, dv = state_ref.shape
    n_chunks = q_ref.shape[0] // c

    @pl.when(pl.program_id(2) == 0)
    def _():
        state_ref[...] = jnp.zeros_like(state_ref)
        ltri_ref[...] = jnp.where(_causal(c), 1.0, 0.0).astype(BF16)

    def chunk(i, hd, direct):
        rows = pl.ds(i * c, c)
        kcols = slice(hd * dk, (hd + 1) * dk)
        vcols = slice(hd * dv, (hd + 1) * dv)
        ltri = ltri_ref[...]
        la = la_ref[rows, kcols]
        a_hi, a_mid, a_lo = _split3(la)
        cum = _dot(ltri, a_hi) + _dot(ltri, a_mid) + _dot(ltri, a_lo)
        tot_row = jnp.sum(la, axis=0, keepdims=True)
        tot_col = jnp.broadcast_to(tot_row, (128, dk)).T
        total = cum[c - 1:c, :]
        yield

        qf = q_ref[rows, kcols].astype(F32) * q_scale
        kf = k_ref[rows, kcols].astype(F32)
        v = v_ref[rows, vcols]
        qs = (qf * jnp.exp(cum)).astype(BF16)
        ks = (kf * jnp.exp(total - cum)).astype(BF16)

        if direct:
            kn = (kf * jnp.exp(-cum)).astype(BF16)
            s = lax.dot_general(qs, kn, _NT, preferred_element_type=F32)
        else:
            cum_ref[...] = cum
            kf_ref[...] = kf
            sc_ref[...] = jnp.zeros_like(sc_ref)
            lane = lax.broadcasted_iota(jnp.int32, (1, c), 1)

            def column(s, carry):
                d = jnp.exp(jnp.minimum(cum_ref[...] - cum_ref[pl.ds(s, 1), :], 0.0))
                colv = jnp.sum(qf * kf_ref[pl.ds(s, 1), :] * d, axis=1, keepdims=True)
                sc_ref[...] += colv * jnp.where(lane == s, 1.0, 0.0)
                return carry

            lax.fori_loop(0, c, column, 0)
            s = sc_ref[...]
        yield
        p = jnp.where(_causal(c), s, 0.0).astype(BF16)

        st = state_ref[hd]
        o = _dot(jnp.concatenate([qs, p], axis=1), jnp.concatenate([st.astype(BF16), v], axis=0))
        yield
        decay = jnp.exp(tot_col)
        decay = jnp.concatenate([decay] * (dv // 128), axis=1)
        state_ref[hd] = st * decay + lax.dot_general(ks, v, _TN, preferred_element_type=F32)

        yield
        ms = jnp.mean(o * o, axis=-1, keepdims=True)
        on = o * lax.rsqrt(ms + EPS)
        o_ref[rows, vcols] = (on * hg_ref[:, vcols]).astype(BF16)

    def step(direct):
        _staggered([chunk(i, hd, direct) for i in range(n_chunks) for hd in range(n_heads)])

    direct_ok = jnp.min(least_ref[...]) >= -FAST_DECAY_LIMIT

    @pl.when(direct_ok)
    def _():
        step(True)

    @pl.when(jnp.logical_not(direct_ok))
    def _():
        step(False)


def _gla_recurrence(proj, la, least, head_g, *, batch, seq, dk_total, dv_total, least_rows):
    m = proj.shape[0]
    hps = HEADS_PER_STEP
    hk = dk_total // GLA_HEADS
    hv = dv_total // GLA_HEADS
    t_rows = CHUNK * GLA_CHUNKS_PER_STEP
    nt = seq // t_rows
    k_off = dk_total // (hps * hk)
    v_off = 2 * dk_total // (hps * hv)
    rows = lambda b, h, t: b * nt + t
    assert least_rows % t_rows == 0
    return pl.pallas_call(
        functools.partial(_gla_rec_kernel, q_scale=float(hk) ** -0.5),
        grid=(batch, GLA_HEADS // hps, nt),
        in_specs=[
            pl.BlockSpec((t_rows, hps * hk), lambda b, h, t: (rows(b, h, t), h)),
            pl.BlockSpec((t_rows, hps * hk), lambda b, h, t: (rows(b, h, t), k_off + h)),
            pl.BlockSpec((t_rows, hps * hv), lambda b, h, t: (rows(b, h, t), v_off + h)),
            pl.BlockSpec((t_rows, hps * hk), lambda b, h, t: (rows(b, h, t), h)),
            pl.BlockSpec((8, hps * hk), lambda b, h, t: (rows(b, h, t) // (least_rows // t_rows), h)),
            pl.BlockSpec((1, hps * hv), lambda b, h, t: (0, h)),
        ],
        out_specs=pl.BlockSpec((t_rows, hps * hv), lambda b, h, t: (rows(b, h, t), h)),
        out_shape=jax.ShapeDtypeStruct((m, dv_total), BF16),
        scratch_shapes=[
            pltpu.VMEM((hps, hk, hv), F32),
            pltpu.VMEM((CHUNK, CHUNK), BF16),
            pltpu.VMEM((CHUNK, hk), F32),
            pltpu.VMEM((CHUNK, hk), F32),
            pltpu.VMEM((CHUNK, CHUNK), F32),
        ],
        compiler_params=pltpu.CompilerParams(
            dimension_semantics=("arbitrary", "arbitrary", "arbitrary"), vmem_limit_bytes=VMEM_LIMIT_BYTES),
        name="gla_recurrence",
    )(proj, proj, proj, la, least, head_g)


def _gated_out(a_ref, gate_refs, w_ref, res_ref):
    gw = gate_refs[0].shape[1]
    y = res_ref[...]
    for b, g_ref in enumerate(gate_refs):
        cols = slice(b * gw, (b + 1) * gw)
        gated = (a_ref[:, cols].astype(F32) * _silu(g_ref[...].astype(F32))).astype(BF16)
        y = y + _dot(gated, w_ref[cols, :])
    return y


def _out_proj_final_kernel(a_ref, *refs):
    *gate_refs, w_ref, res_ref, g_ref, o_ref = refs
    o_ref[...] = _rms_rows(_gated_out(a_ref, gate_refs, w_ref, res_ref), g_ref[...])


def _out_proj_next_kernel(a_ref, *refs):
    *gate_refs, w_ref, res_ref, g_ref, pos_ref, invf_ref, o_ref, hn_ref, cos_ref, sin_ref = refs
    y = _gated_out(a_ref, gate_refs, w_ref, res_ref)
    o_ref[...] = y
    hn_ref[...] = _rms_rows(y, g_ref[...]).astype(BF16)
    ang = pos_ref[...].astype(F32) * invf_ref[...]
    cos_ref[...] = jnp.cos(ang)
    sin_ref[...] = jnp.sin(ang)


def _out_proj(a, proj, gate_col, w, res, norm_g, pos=None, inv_freq=None, *, final, tm):
    m, k = a.shape
    n = w.shape[1]
    gw = math.gcd(gate_col, k)
    gate_specs = [pl.BlockSpec((tm, gw), functools.partial(lambda i, c: (i, c), c=gate_col // gw + b))
                  for b in range(k // gw)]
    row_spec = pl.BlockSpec((tm, n), lambda i: (i, 0))
    in_specs = [pl.BlockSpec((tm, k), lambda i: (i, 0))] + gate_specs + [
        pl.BlockSpec((k, n), lambda i: (0, 0), pipeline_mode=pl.Buffered(1)),
        row_spec,
        pl.BlockSpec((1, n), lambda i: (0, 0)),
    ]
    args = (a,) + (proj,) * len(gate_specs) + (w, res, norm_g)
    y_shape = jax.ShapeDtypeStruct((m, n), F32)
    if final:
        out_specs, out_shape = row_spec, y_shape
    else:
        half = inv_freq.shape[1]
        in_specs += [pl.BlockSpec((tm, 1), lambda i: (i, 0)), pl.BlockSpec((1, half), lambda i: (0, 0))]
        args += (pos, inv_freq)
        rot_spec = pl.BlockSpec((tm, half), lambda i: (i, 0))
        rot_shape = jax.ShapeDtypeStruct((m, half), F32)
        out_specs = [row_spec, row_spec, rot_spec, rot_spec]
        out_shape = [y_shape, jax.ShapeDtypeStruct((m, n), BF16), rot_shape, rot_shape]
    return pl.pallas_call(
        _out_proj_final_kernel if final else _out_proj_next_kernel,
        grid=(m // tm,),
        in_specs=in_specs,
        out_specs=out_specs,
        out_shape=out_shape,
        compiler_params=pltpu.CompilerParams(
            dimension_semantics=("arbitrary",), vmem_limit_bytes=LARGE_VMEM_LIMIT_BYTES),
        name="out_proj_final" if final else "out_proj_next",
    )(*args)


def _ret_in_kernel(hn_ref, cos_ref, sin_ref, w_ref, proj_ref, *, n_q_blocks, n_k_blocks, hk, k_scale):
    j = pl.program_id(1)
    half = hk // 2

    @pl.when(j < n_q_blocks + n_k_blocks)
    def _():
        scale = jnp.where(j >= n_q_blocks, k_scale, 1.0).astype(F32)
        cos = cos_ref[...] * scale
        sin = sin_ref[...] * scale
        for c0 in range(0, w_ref.shape[1], PROJ_PIECE):
            acc = _dot(hn_ref[...], w_ref[:, c0:c0 + PROJ_PIECE])
            for h0 in range(0, PROJ_PIECE, hk):
                t1 = acc[:, h0:h0 + half]
                t2 = acc[:, h0 + half:h0 + hk]
                proj_ref[:, c0 + h0:c0 + h0 + half] = (t1 * cos - t2 * sin).astype(BF16)
                proj_ref[:, c0 + h0 + half:c0 + h0 + hk] = (t2 * cos + t1 * sin).astype(BF16)

    @pl.when(j >= n_q_blocks + n_k_blocks)
    def _():
        for c0 in range(0, w_ref.shape[1], PROJ_PIECE):
            cols = slice(c0, c0 + PROJ_PIECE)
            proj_ref[:, cols] = _dot(hn_ref[...], w_ref[:, cols]).astype(BF16)


def _ret_in_proj(hn, cos, sin, w, *, dk_total, tm, tn):
    m, d = hn.shape
    n = w.shape[1]
    hk = dk_total // RET_HEADS
    half = hk // 2
    assert dk_total % tn == 0
    kern = functools.partial(_ret_in_kernel, n_q_blocks=dk_total // tn, n_k_blocks=dk_total // tn,
                             hk=hk, k_scale=float(hk) ** -0.5)
    return pl.pallas_call(
        kern,
        grid=(m // tm, n // tn),
        in_specs=[
            pl.BlockSpec((tm, d), lambda i, j: (i, 0)),
            pl.BlockSpec((tm, half), lambda i, j: (i, 0)),
            pl.BlockSpec((tm, half), lambda i, j: (i, 0)),
            pl.BlockSpec((d, tn), lambda i, j: (0, j)),
        ],
        out_specs=pl.BlockSpec((tm, tn), lambda i, j: (i, j)),
        out_shape=jax.ShapeDtypeStruct((m, n), BF16),
        compiler_params=pltpu.CompilerParams(
            dimension_semantics=("arbitrary", "arbitrary"), vmem_limit_bytes=VMEM_LIMIT_BYTES),
        name="ret_in_proj",
    )(hn, cos, sin, w)


def _ret_rec_kernel(q_ref, k_ref, v_ref, lg_ref, gng_ref, gnb_ref, o_ref,
                    state_ref, dmat_ref, xi_ref, zeta_ref):
    c = CHUNK
    n_heads, dk, dv = state_ref.shape

    @pl.when(pl.program_id(2) == 0)
    def _():
        state_ref[...] = jnp.zeros_like(state_ref)
        row = lax.broadcasted_iota(jnp.int32, (c, c), 0)
        col = lax.broadcasted_iota(jnp.int32, (c, c), 1)
        dpos = (row - col).astype(F32)
        idx_v = lax.broadcasted_iota(jnp.int32, (c, dv), 0).astype(F32)
        idx_k = lax.broadcasted_iota(jnp.int32, (c, dk), 0).astype(F32)
        for hd in range(n_heads):
            dmat_ref[hd] = jnp.where(dpos >= 0, jnp.exp(lg_ref[hd, :, :c] * jnp.maximum(dpos, 0.0)), 0.0)
            xi_ref[hd] = jnp.exp(lg_ref[hd, :, :dv] * (idx_v + 1.0))
            zeta_ref[hd] = jnp.exp(lg_ref[hd, :, :dk] * (c - 1.0 - idx_k))

    def chunk(i, hd):
        rows = pl.ds(i * c, c)
        kcols = slice(hd * dk, (hd + 1) * dk)
        vcols = slice(hd * dv, (hd + 1) * dv)
        q = q_ref[rows, kcols]
        k = k_ref[rows, kcols]
        v = v_ref[rows, vcols]
        s = lax.dot_general(q, k, _NT, preferred_element_type=F32) * dmat_ref[hd]
        st = state_ref[hd]
        o = _dot(s.astype(BF16), v) + _dot(q, st.astype(BF16)) * xi_ref[hd]
        kz = (k.astype(F32) * zeta_ref[hd]).astype(BF16)
        state_ref[hd] = (st * jnp.exp(lg_ref[hd, :, :dv] * float(c))
                         + lax.dot_general(kz, v, _TN, preferred_element_type=F32))

        mu = jnp.mean(o, axis=-1, keepdims=True)
        d = o - mu
        var = jnp.mean(d * d, axis=-1, keepdims=True)
        on = d * lax.rsqrt(var + EPS)
        o_ref[rows, vcols] = (on * gng_ref[:, vcols] + gnb_ref[:, vcols]).astype(BF16)

    for i in range(q_ref.shape[0] // c):
        for hd in range(n_heads):
            chunk(i, hd)


def _ret_recurrence(proj, log_gamma, gn_g, gn_b, *, batch, seq, dk_total, dv_total):
    m = proj.shape[0]
    hps = HEADS_PER_STEP
    hk = dk_total // RET_HEADS
    hv = dv_total // RET_HEADS
    t_rows = CHUNK * RET_CHUNKS_PER_STEP
    nt = seq // t_rows
    width = log_gamma.shape[-1]
    k_off = dk_total // (hps * hk)
    v_off = 2 * dk_total // (hps * hv)
    rows = lambda b, h, t: b * nt + t
    return pl.pallas_call(
        _ret_rec_kernel,
        grid=(batch, RET_HEADS // hps, nt),
        in_specs=[
            pl.BlockSpec((t_rows, hps * hk), lambda b, h, t: (rows(b, h, t), h)),
            pl.BlockSpec((t_rows, hps * hk), lambda b, h, t: (rows(b, h, t), k_off + h)),
            pl.BlockSpec((t_rows, hps * hv), lambda b, h, t: (rows(b, h, t), v_off + h)),
            pl.BlockSpec((hps, 1, width), lambda b, h, t: (h, 0, 0)),
            pl.BlockSpec((1, hps * hv), lambda b, h, t: (0, h)),
            pl.BlockSpec((1, hps * hv), lambda b, h, t: (0, h)),
        ],
        out_specs=pl.BlockSpec((t_rows, hps * hv), lambda b, h, t: (rows(b, h, t), h)),
        out_shape=jax.ShapeDtypeStruct((m, dv_total), BF16),
        scratch_shapes=[
            pltpu.VMEM((hps, hk, hv), F32),
            pltpu.VMEM((hps, CHUNK, CHUNK), F32),
            pltpu.VMEM((hps, CHUNK, hv), F32),
            pltpu.VMEM((hps, CHUNK, hk), F32),
        ],
        compiler_params=pltpu.CompilerParams(
            dimension_semantics=("arbitrary", "arbitrary", "arbitrary"), vmem_limit_bytes=VMEM_LIMIT_BYTES),
        name="ret_recurrence",
    )(proj, proj, proj, log_gamma, gn_g, gn_b)


GLA_IN_PROJ_TILE = (1024, 2048)
RET_IN_PROJ_TILE = (1024, 2048)
OUT_PROJ_TM = 512


def _gla_layer(h, batch, seq, norm_g, w_in, w_a1, w_a2, b_a, head_g, w_out, out_norm_g, pos, inv_freq, final):
    d = h.shape[1]
    dk_total = w_a2.shape[1]
    dv_total = w_out.shape[0]
    rank = w_a1.shape[1]
    wa1 = jnp.zeros((d, GLA_GATE_RANK_PAD), BF16).at[:, :rank].set(w_a1.astype(BF16))
    wa2 = jnp.zeros((GLA_GATE_RANK_PAD, dk_total), BF16).at[:rank, :].set(w_a2.astype(BF16))
    tm, tn = GLA_IN_PROJ_TILE
    proj, la, least = _gla_in_proj(h, norm_g[None, :], w_in.astype(BF16), wa1, wa2, b_a[None, :], tm=tm, tn=tn)
    o = _gla_recurrence(proj, la, least, head_g[None, :], batch=batch, seq=seq, dk_total=dk_total,
                        dv_total=dv_total, least_rows=tm)
    return _out_proj(o, proj, 2 * dk_total + dv_total, w_out.astype(BF16), h, out_norm_g, pos, inv_freq,
                     final=final, tm=OUT_PROJ_TM)


def _ret_layer(h, hn, cos, sin, batch, seq, w_in, gn_g, gn_b, w_out, out_norm_g, pos, inv_freq, final):
    dv_total = w_out.shape[0]
    dk_total = (w_in.shape[1] - 2 * dv_total) // 2
    hk = dk_total // RET_HEADS
    log_gamma = jnp.log1p(-jnp.exp2(-5.0 - jnp.arange(RET_HEADS, dtype=F32)))
    width = max(CHUNK, hk, dv_total // RET_HEADS)
    log_gamma = jnp.broadcast_to(log_gamma[:, None, None], (RET_HEADS, 1, width))
    tm, tn = RET_IN_PROJ_TILE
    proj = _ret_in_proj(hn, cos, sin, w_in.astype(BF16), dk_total=dk_total, tm=tm, tn=tn)
    o = _ret_recurrence(proj, log_gamma, gn_g[None, :], gn_b[None, :],
                        batch=batch, seq=seq, dk_total=dk_total, dv_total=dv_total)
    return _out_proj(o, proj, 2 * dk_total + dv_total, w_out.astype(BF16), h, out_norm_g, pos, inv_freq,
                     final=final, tm=OUT_PROJ_TM)


def kernel(x, positions, gla_norm, gla_w_in, gla_w_a1, gla_w_a2, gla_b_a, gla_head_g, gla_w_out,
           ret_norm, ret_w_in, ret_gn_g, ret_gn_b, ret_w_out, final_norm):
    batch, seq, d = x.shape
    m = batch * seq
    h = x.reshape(m, d)
    pos = positions.reshape(m, 1)
    ret_half = (ret_w_in.shape[2] - 2 * ret_w_out.shape[1]) // (4 * RET_HEADS)
    inv_freq = (ROPE_BASE ** (-jnp.arange(ret_half, dtype=F32) / ret_half))[None, :]
    depth = gla_norm.shape[0] + ret_norm.shape[0]
    assert depth >= 1
    hn = cos = sin = None
    for i in range(depth):
        j = i // 2
        final = i == depth - 1
        if final:
            out_norm_g = final_norm
        else:
            out_norm_g = ret_norm[j] if i % 2 == 0 else gla_norm[j + 1]
        if i % 2 == 0:
            out = _gla_layer(h, batch, seq, gla_norm[j], gla_w_in[j], gla_w_a1[j], gla_w_a2[j], gla_b_a[j],
                             gla_head_g[j], gla_w_out[j], out_norm_g[None, :], pos, inv_freq, final)
        else:
            out = _ret_layer(h, hn, cos, sin, batch, seq, ret_w_in[j], ret_gn_g[j], ret_gn_b[j], ret_w_out[j],
                             out_norm_g[None, :], pos, inv_freq, final)
        if final:
            h = out
        else:
            h, hn, cos, sin = out
    return h.reshape(batch, seq, d)
```

```python
import functools
import math

import jax
import jax.numpy as jnp
from jax import lax
from jax.experimental import pallas as pl
from jax.experimental.pallas import tpu as pltpu

F32 = jnp.float32
BF16 = jnp.bfloat16

EPS = 1e-6
GLA_HEADS = 4
GLA_GATE_RANK_PAD = 128
GLA_GATE_TEMP = 16.0
RET_HEADS = 8
ROPE_BASE = 10000.0

CHUNK = 256
HEADS_PER_STEP = 2
GLA_CHUNKS_PER_STEP = 4
RET_CHUNKS_PER_STEP = 4
FAST_DECAY_LIMIT = 60.0
VMEM_LIMIT_BYTES = 56 * 1024 * 1024
PROJ_PIECE = 1024
LARGE_VMEM_LIMIT_BYTES = 60 * 1024 * 1024

_NT = (((1,), (1,)), ((), ()))
_TN = (((0,), (0,)), ((), ()))


def _dot(a, b):
    return jnp.dot(a, b, preferred_element_type=F32)


def _rms_rows(x, g):
    ms = jnp.mean(x * x, axis=-1, keepdims=True)
    return x * lax.rsqrt(ms + EPS) * g


def _silu(g):
    hg = 0.5 * g
    return hg + hg * jnp.tanh(hg)


def _split3(a):
    hi = a.astype(BF16)
    r1 = a - hi.astype(F32)
    mid = r1.astype(BF16)
    lo = (r1 - mid.astype(F32)).astype(BF16)
    return hi, mid, lo


def _causal(c):
    row = lax.broadcasted_iota(jnp.int32, (c, c), 0)
    col = lax.broadcasted_iota(jnp.int32, (c, c), 1)
    return row >= col


def _staggered(bodies):
    started = []
    pending = list(bodies)
    while pending or started:
        if pending:
            started.append(pending.pop(0))
        for body in list(reversed(started)):
            try:
                next(body)
            except StopIteration:
                started.remove(body)


def _gla_in_kernel(x_ref, g_ref, w_ref, wa1_ref, wa2_ref, ba_ref, proj_ref, la_ref, least_ref,
                   hn_ref, rs_ref, z_ref):
    j = pl.program_id(1)

    def project():
        rs = rs_ref[...]
        for c0 in range(0, w_ref.shape[1], PROJ_PIECE):
            cols = slice(c0, c0 + PROJ_PIECE)
            proj_ref[:, cols] = (_dot(hn_ref[...], w_ref[:, cols]) * rs).astype(BF16)

    @pl.when(j == 0)
    def _():
        x = x_ref[...]
        hn = (x * g_ref[...]).astype(BF16)
        hn_ref[...] = hn
        rs = lax.rsqrt(jnp.mean(x * x, axis=-1, keepdims=True) + EPS)
        rs_ref[...] = rs
        z_ref[...] = (_dot(hn, wa1_ref[...]) * rs).astype(BF16)
        project()

    @pl.when(j == 1)
    def _():
        xa = _dot(z_ref[...], wa2_ref[...]) + ba_ref[...]
        la = (jnp.minimum(xa, 0.0) - jnp.log(1.0 + jnp.exp(-jnp.abs(xa)))) * (1.0 / GLA_GATE_TEMP)
        la_ref[...] = la
        least = jnp.sum(la[:CHUNK], axis=0, keepdims=True)
        for c0 in range(CHUNK, la.shape[0], CHUNK):
            least = jnp.minimum(least, jnp.sum(la[c0:c0 + CHUNK], axis=0, keepdims=True))
        least_ref[...] = jnp.broadcast_to(least, least_ref.shape)
        project()

    @pl.when(j > 1)
    def _():
        project()


def _gla_in_proj(x2, g, w, wa1, wa2, ba, *, tm, tn):
    m, d = x2.shape
    n = w.shape[1]
    dk = wa2.shape[1]
    assert n // tn >= 2
    assert tm % CHUNK == 0
    return pl.pallas_call(
        _gla_in_kernel,
        grid=(m // tm, n // tn),
        in_specs=[
            pl.BlockSpec((tm, d), lambda i, j: (i, 0)),
            pl.BlockSpec((1, d), lambda i, j: (0, 0)),
            pl.BlockSpec((d, tn), lambda i, j: (0, j)),
            pl.BlockSpec((d, GLA_GATE_RANK_PAD), lambda i, j: (0, 0)),
            pl.BlockSpec((GLA_GATE_RANK_PAD, dk), lambda i, j: (0, 0)),
            pl.BlockSpec((1, dk), lambda i, j: (0, 0)),
        ],
        out_specs=[
            pl.BlockSpec((tm, tn), lambda i, j: (i, j)),
            pl.BlockSpec((tm, dk), lambda i, j: (i, 0)),
            pl.BlockSpec((8, dk), lambda i, j: (i, 0)),
        ],
        out_shape=[
            jax.ShapeDtypeStruct((m, n), BF16),
            jax.ShapeDtypeStruct((m, dk), F32),
            jax.ShapeDtypeStruct((m // tm * 8, dk), F32),
        ],
        scratch_shapes=[pltpu.VMEM((tm, d), BF16), pltpu.VMEM((tm, 1), F32), pltpu.VMEM((tm, GLA_GATE_RANK_PAD), BF16)],
        compiler_params=pltpu.CompilerParams(
            dimension_semantics=("arbitrary", "arbitrary"), vmem_limit_bytes=LARGE_VMEM_LIMIT_BYTES),
        name="gla_in_proj",
    )(x2, g, w, wa1, wa2, ba)


def _gla_rec_kernel(q_ref, k_ref, v_ref, la_ref, least_ref, hg_ref, o_ref,
                    state_ref, ltri_ref, cum_ref, kf_ref, sc_ref, *, q_scale):
    c = CHUNK
    n_heads, dk, dv = state_ref.shape
    n_chunks = q_ref.shape[0] // c

    @pl.when(pl.program_id(2) == 0)
    def _():
        state_ref[...] = jnp.zeros_like(state_ref)
        ltri_ref[...] = jnp.where(_causal(c), 1.0, 0.0).astype(BF16)

    def chunk(i, hd, direct):
        rows = pl.ds(i * c, c)
        kcols = slice(hd * dk, (hd + 1) * dk)
        vcols = slice(hd * dv, (hd + 1) * dv)
        ltri = ltri_ref[...]
        la = la_ref[rows, kcols]
        a_hi, a_mid, a_lo = _split3(la)
        cum = _dot(ltri, a_hi) + _dot(ltri, a_mid) + _dot(ltri, a_lo)
        tot_row = jnp.sum(la, axis=0, keepdims=True)
        tot_col = jnp.broadcast_to(tot_row, (128, dk)).T
        total = cum[c - 1:c, :]
        yield

        qf = q_ref[rows, kcols].astype(F32) * q_scale
        kf = k_ref[rows, kcols].astype(F32)
        v = v_ref[rows, vcols]
        qs = (qf * jnp.exp(cum)).astype(BF16)
        ks = (kf * jnp.exp(total - cum)).astype(BF16)

        if direct:
            kn = (kf * jnp.exp(-cum)).astype(BF16)
            s = lax.dot_general(qs, kn, _NT, preferred_element_type=F32)
        else:
            cum_ref[...] = cum
            kf_ref[...] = kf
            sc_ref[...] = jnp.zeros_like(sc_ref)
            lane = lax.broadcasted_iota(jnp.int32, (1, c), 1)

            def column(s, carry):
                d = jnp.exp(jnp.minimum(cum_ref[...] - cum_ref[pl.ds(s, 1), :], 0.0))
                colv = jnp.sum(qf * kf_ref[pl.ds(s, 1), :] * d, axis=1, keepdims=True)
                sc_ref[...] += colv * jnp.where(lane == s, 1.0, 0.0)
                return carry

            lax.fori_loop(0, c, column, 0)
            s = sc_ref[...]
        yield
        p = jnp.where(_causal(c), s, 0.0).astype(BF16)

        st = state_ref[hd]
        o = _dot(jnp.concatenate([qs, p], axis=1), jnp.concatenate([st.astype(BF16), v], axis=0))
        yield
        decay = jnp.exp(tot_col)
        decay = jnp.concatenate([decay] * (dv // 128), axis=1)
        state_ref[hd] = st * decay + lax.dot_general(ks, v, _TN, preferred_element_type=F32)

        yield
        ms = jnp.mean(o * o, axis=-1, keepdims=True)
        on = o * lax.rsqrt(ms + EPS)
        o_ref[rows, vcols] = (on * hg_ref[:, vcols]).astype(BF16)

    def step(direct):
        _staggered([chunk(i, hd, direct) for i in range(n_chunks) for hd in range(n_heads)])

    direct_ok = jnp.min(least_ref[...]) >= -FAST_DECAY_LIMIT

    @pl.when(direct_ok)
    def _():
        step(True)

    @pl.when(jnp.logical_not(direct_ok))
    def _():
        step(False)


def _gla_recurrence(proj, la, least, head_g, *, batch, seq, dk_total, dv_total, least_rows):
    m = proj.shape[0]
    hps = HEADS_PER_STEP
    hk = dk_total // GLA_HEADS
    hv = dv_total // GLA_HEADS
    t_rows = CHUNK * GLA_CHUNKS_PER_STEP
    nt = seq // t_rows
    k_off = dk_total // (hps * hk)
    v_off = 2 * dk_total // (hps * hv)
    rows = lambda b, h, t: b * nt + t
    assert least_rows % t_rows == 0
    return pl.pallas_call(
        functools.partial(_gla_rec_kernel, q_scale=float(hk) ** -0.5),
        grid=(batch, GLA_HEADS // hps, nt),
        in_specs=[
            pl.BlockSpec((t_rows, hps * hk), lambda b, h, t: (rows(b, h, t), h)),
            pl.BlockSpec((t_rows, hps * hk), lambda b, h, t: (rows(b, h, t), k_off + h)),
            pl.BlockSpec((t_rows, hps * hv), lambda b, h, t: (rows(b, h, t), v_off + h)),
            pl.BlockSpec((t_rows, hps * hk), lambda b, h, t: (rows(b, h, t), h)),
            pl.BlockSpec((8, hps * hk), lambda b, h, t: (rows(b, h, t) // (least_rows // t_rows), h)),
            pl.BlockSpec((1, hps * hv), lambda b, h, t: (0, h)),
        ],
        out_specs=pl.BlockSpec((t_rows, hps * hv), lambda b, h, t: (rows(b, h, t), h)),
        out_shape=jax.ShapeDtypeStruct((m, dv_total), BF16),
        scratch_shapes=[
            pltpu.VMEM((hps, hk, hv), F32),
            pltpu.VMEM((CHUNK, CHUNK), BF16),
            pltpu.VMEM((CHUNK, hk), F32),
            pltpu.VMEM((CHUNK, hk), F32),
            pltpu.VMEM((CHUNK, CHUNK), F32),
        ],
        compiler_params=pltpu.CompilerParams(
            dimension_semantics=("arbitrary", "arbitrary", "arbitrary"), vmem_limit_bytes=VMEM_LIMIT_BYTES),
        name="gla_recurrence",
    )(proj, proj, proj, la, least, head_g)


def _gated_out(a_ref, gate_refs, w_ref, res_ref):
    gw = gate_refs[0].shape[1]
    y = res_ref[...]
    for b, g_ref in enumerate(gate_refs):
        cols = slice(b * gw, (b + 1) * gw)
        gated = (a_ref[:, cols].astype(F32) * _silu(g_ref[...].astype(F32))).astype(BF16)
        y = y + _dot(gated, w_ref[cols, :])
    return y


def _out_proj_final_kernel(a_ref, *refs):
    *gate_refs, w_ref, res_ref, g_ref, o_ref = refs
    o_ref[...] = _rms_rows(_gated_out(a_ref, gate_refs, w_ref, res_ref), g_ref[...])


def _out_proj_next_kernel(a_ref, *refs):
    *gate_refs, w_ref, res_ref, g_ref, pos_ref, invf_ref, o_ref, hn_ref, cos_ref, sin_ref = refs
    y = _gated_out(a_ref, gate_refs, w_ref, res_ref)
    o_ref[...] = y
    hn_ref[...] = _rms_rows(y, g_ref[...]).astype(BF16)
    ang = pos_ref[...].astype(F32) * invf_ref[...]
    cos_ref[...] = jnp.cos(ang)
    sin_ref[...] = jnp.sin(ang)


def _out_proj(a, proj, gate_col, w, res, norm_g, pos=None, inv_freq=None, *, final, tm):
    m, k = a.shape
    n = w.shape[1]
    gw = math.gcd(gate_col, k)
    gate_specs = [pl.BlockSpec((tm, gw), functools.partial(lambda i, c: (i, c), c=gate_col // gw + b))
                  for b in range(k // gw)]
    row_spec = pl.BlockSpec((tm, n), lambda i: (i, 0))
    in_specs = [pl.BlockSpec((tm, k), lambda i: (i, 0))] + gate_specs + [
        pl.BlockSpec((k, n), lambda i: (0, 0), pipeline_mode=pl.Buffered(1)),
        row_spec,
        pl.BlockSpec((1, n), lambda i: (0, 0)),
    ]
    args = (a,) + (proj,) * len(gate_specs) + (w, res, norm_g)
    y_shape = jax.ShapeDtypeStruct((m, n), F32)
    if final:
        out_specs, out_shape = row_spec, y_shape
    else:
        half = inv_freq.shape[1]
        in_specs += [pl.BlockSpec((tm, 1), lambda i: (i, 0)), pl.BlockSpec((1, half), lambda i: (0, 0))]
        args += (pos, inv_freq)
        rot_spec = pl.BlockSpec((tm, half), lambda i: (i, 0))
        rot_shape = jax.ShapeDtypeStruct((m, half), F32)
        out_specs = [row_spec, row_spec, rot_spec, rot_spec]
        out_shape = [y_shape, jax.ShapeDtypeStruct((m, n), BF16), rot_shape, rot_shape]
    return pl.pallas_call(
        _out_proj_final_kernel if final else _out_proj_next_kernel,
        grid=(m // tm,),
        in_specs=in_specs,
        out_specs=out_specs,
        out_shape=out_shape,
        compiler_params=pltpu.CompilerParams(
            dimension_semantics=("arbitrary",), vmem_limit_bytes=LARGE_VMEM_LIMIT_BYTES),
        name="out_proj_final" if final else "out_proj_next",
    )(*args)


def _ret_in_kernel(hn_ref, cos_ref, sin_ref, w_ref, proj_ref, *, n_q_blocks, n_k_blocks, hk, k_scale):
    j = pl.program_id(1)
    half = hk // 2

    @pl.when(j < n_q_blocks + n_k_blocks)
    def _():
        scale = jnp.where(j >= n_q_blocks, k_scale, 1.0).astype(F32)
        cos = cos_ref[...] * scale
        sin = sin_ref[...] * scale
        for c0 in range(0, w_ref.shape[1], PROJ_PIECE):
            acc = _dot(hn_ref[...], w_ref[:, c0:c0 + PROJ_PIECE])
            for h0 in range(0, PROJ_PIECE, hk):
                t1 = acc[:, h0:h0 + half]
                t2 = acc[:, h0 + half:h0 + hk]
                proj_ref[:, c0 + h0:c0 + h0 + half] = (t1 * cos - t2 * sin).astype(BF16)
                proj_ref[:, c0 + h0 + half:c0 + h0 + hk] = (t2 * cos + t1 * sin).astype(BF16)

    @pl.when(j >= n_q_blocks + n_k_blocks)
    def _():
        for c0 in range(0, w_ref.shape[1], PROJ_PIECE):
            cols = slice(c0, c0 + PROJ_PIECE)
            proj_ref[:, cols] = _dot(hn_ref[...], w_ref[:, cols]).astype(BF16)


def _ret_in_proj(hn, cos, sin, w, *, dk_total, tm, tn):
    m, d = hn.shape
    n = w.shape[1]
    hk = dk_total // RET_HEADS
    half = hk // 2
    assert dk_total % tn == 0
    kern = functools.partial(_ret_in_kernel, n_q_blocks=dk_total // tn, n_k_blocks=dk_total // tn,
                             hk=hk, k_scale=float(hk) ** -0.5)
    return pl.pallas_call(
        kern,
        grid=(m // tm, n // tn),
        in_specs=[
            pl.BlockSpec((tm, d), lambda i, j: (i, 0)),
            pl.BlockSpec((tm, half), lambda i, j: (i, 0)),
            pl.BlockSpec((tm, half), lambda i, j: (i, 0)),
            pl.BlockSpec((d, tn), lambda i, j: (0, j)),
        ],
        out_specs=pl.BlockSpec((tm, tn), lambda i, j: (i, j)),
        out_shape=jax.ShapeDtypeStruct((m, n), BF16),
        compiler_params=pltpu.CompilerParams(
            dimension_semantics=("arbitrary", "arbitrary"), vmem_limit_bytes=VMEM_LIMIT_BYTES),
        name="ret_in_proj",
    )(hn, cos, sin, w)


def _ret_rec_kernel(q_ref, k_ref, v_ref, lg_ref, gng_ref, gnb_ref, o_ref,
                    state_ref, dmat_ref, xi_ref, zeta_ref):
    c = CHUNK
    n_heads, dk, dv = state_ref.shape

    @pl.when(pl.program_id(2) == 0)
    def _():
        state_ref[...] = jnp.zeros_like(state_ref)
        row = lax.broadcasted_iota(jnp.int32, (c, c), 0)
        col = lax.broadcasted_iota(jnp.int32, (c, c), 1)
        dpos = (row - col).astype(F32)
        idx_v = lax.broadcasted_iota(jnp.int32, (c, dv), 0).astype(F32)
        idx_k = lax.broadcasted_iota(jnp.int32, (c, dk), 0).astype(F32)
        for hd in range(n_heads):
            dmat_ref[hd] = jnp.where(dpos >= 0, jnp.exp(lg_ref[hd, :, :c] * jnp.maximum(dpos, 0.0)), 0.0)
            xi_ref[hd] = jnp.exp(lg_ref[hd, :, :dv] * (idx_v + 1.0))
            zeta_ref[hd] = jnp.exp(lg_ref[hd, :, :dk] * (c - 1.0 - idx_k))

    def chunk(i, hd):
        rows = pl.ds(i * c, c)
        kcols = slice(hd * dk, (hd + 1) * dk)
        vcols = slice(hd * dv, (hd + 1) * dv)
        q = q_ref[rows, kcols]
        k = k_ref[rows, kcols]
        v = v_ref[rows, vcols]
        s = lax.dot_general(q, k, _NT, preferred_element_type=F32) * dmat_ref[hd]
        st = state_ref[hd]
        o = _dot(s.astype(BF16), v) + _dot(q, st.astype(BF16)) * xi_ref[hd]
        kz = (k.astype(F32) * zeta_ref[hd]).astype(BF16)
        state_ref[hd] = (st * jnp.exp(lg_ref[hd, :, :dv] * float(c))
                         + lax.dot_general(kz, v, _TN, preferred_element_type=F32))

        mu = jnp.mean(o, axis=-1, keepdims=True)
        d = o - mu
        var = jnp.mean(d * d, axis=-1, keepdims=True)
        on = d * lax.rsqrt(var + EPS)
        o_ref[rows, vcols] = (on * gng_ref[:, vcols] + gnb_ref[:, vcols]).astype(BF16)

    for i in range(q_ref.shape[0] // c):
        for hd in range(n_heads):
            chunk(i, hd)


def _ret_recurrence(proj, log_gamma, gn_g, gn_b, *, batch, seq, dk_total, dv_total):
    m = proj.shape[0]
    hps = HEADS_PER_STEP
    hk = dk_total // RET_HEADS
    hv = dv_total // RET_HEADS
    t_rows = CHUNK * RET_CHUNKS_PER_STEP
    nt = seq // t_rows
    width = log_gamma.shape[-1]
    k_off = dk_total // (hps * hk)
    v_off = 2 * dk_total // (hps * hv)
    rows = lambda b, h, t: b * nt + t
    return pl.pallas_call(
        _ret_rec_kernel,
        grid=(batch, RET_HEADS // hps, nt),
        in_specs=[
            pl.BlockSpec((t_rows, hps * hk), lambda b, h, t: (rows(b, h, t), h)),
            pl.BlockSpec((t_rows, hps * hk), lambda b, h, t: (rows(b, h, t), k_off + h)),
            pl.BlockSpec((t_rows, hps * hv), lambda b, h, t: (rows(b, h, t), v_off + h)),
            pl.BlockSpec((hps, 1, width), lambda b, h, t: (h, 0, 0)),
            pl.BlockSpec((1, hps * hv), lambda b, h, t: (0, h)),
            pl.BlockSpec((1, hps * hv), lambda b, h, t: (0, h)),
        ],
        out_specs=pl.BlockSpec((t_rows, hps * hv), lambda b, h, t: (rows(b, h, t), h)),
        out_shape=jax.ShapeDtypeStruct((m, dv_total), BF16),
        scratch_shapes=[
            pltpu.VMEM((hps, hk, hv), F32),
            pltpu.VMEM((hps, CHUNK, CHUNK), F32),
            pltpu.VMEM((hps, CHUNK, hv), F32),
            pltpu.VMEM((hps, CHUNK, hk), F32),
        ],
        compiler_params=pltpu.CompilerParams(
            dimension_semantics=("arbitrary", "arbitrary", "arbitrary"), vmem_limit_bytes=VMEM_LIMIT_BYTES),
        name="ret_recurrence",
    )(proj, proj, proj, log_gamma, gn_g, gn_b)


GLA_IN_PROJ_TILE = (1024, 2048)
RET_IN_PROJ_TILE = (1024, 2048)
OUT_PROJ_TM = 512


def _gla_layer(h, batch, seq, norm_g, w_in, w_a1, w_a2, b_a, head_g, w_out, out_norm_g, pos, inv_freq, final):
    d = h.shape[1]
    dk_total = w_a2.shape[1]
    dv_total = w_out.shape[0]
    rank = w_a1.shape[1]
    wa1 = jnp.zeros((d, GLA_GATE_RANK_PAD), BF16).at[:, :rank].set(w_a1.astype(BF16))
    wa2 = jnp.zeros((GLA_GATE_RANK_PAD, dk_total), BF16).at[:rank, :].set(w_a2.astype(BF16))
    tm, tn = GLA_IN_PROJ_TILE
    proj, la, least = _gla_in_proj(h, norm_g[None, :], w_in.astype(BF16), wa1, wa2, b_a[None, :], tm=tm, tn=tn)
    o = _gla_recurrence(proj, la, least, head_g[None, :], batch=batch, seq=seq, dk_total=dk_total,
                        dv_total=dv_total, least_rows=tm)
    return _out_proj(o, proj, 2 * dk_total + dv_total, w_out.astype(BF16), h, out_norm_g, pos, inv_freq,
                     final=final, tm=OUT_PROJ_TM)


def _ret_layer(h, hn, cos, sin, batch, seq, w_in, gn_g, gn_b, w_out, out_norm_g, pos, inv_freq, final):
    dv_total = w_out.shape[0]
    dk_total = (w_in.shape[1] - 2 * dv_total) // 2
    hk = dk_total // RET_HEADS
    log_gamma = jnp.log1p(-jnp.exp2(-5.0 - jnp.arange(RET_HEADS, dtype=F32)))
    width = max(CHUNK, hk, dv_total // RET_HEADS)
    log_gamma = jnp.broadcast_to(log_gamma[:, None, None], (RET_HEADS, 1, width))
    tm, tn = RET_IN_PROJ_TILE
    proj = _ret_in_proj(hn, cos, sin, w_in.astype(BF16), dk_total=dk_total, tm=tm, tn=tn)
    o = _ret_recurrence(proj, log_gamma, gn_g[None, :], gn_b[None, :],
                        batch=batch, seq=seq, dk_total=dk_total, dv_total=dv_total)
    return _out_proj(o, proj, 2 * dk_total + dv_total, w_out.astype(BF16), h, out_norm_g, pos, inv_freq,
                     final=final, tm=OUT_PROJ_TM)


def kernel(x, positions, gla_norm, gla_w_in, gla_w_a1, gla_w_a2, gla_b_a, gla_head_g, gla_w_out,
           ret_norm, ret_w_in, ret_gn_g, ret_gn_b, ret_w_out, final_norm):
    batch, seq, d = x.shape
    m = batch * seq
    h = x.reshape(m, d)
    pos = positions.reshape(m, 1)
    ret_half = (ret_w_in.shape[2] - 2 * ret_w_out.shape[1]) // (4 * RET_HEADS)
    inv_freq = (ROPE_BASE ** (-jnp.arange(ret_half, dtype=F32) / ret_half))[None, :]
    depth = gla_norm.shape[0] + ret_norm.shape[0]
    assert depth >= 1
    hn = cos = sin = None
    for i in range(depth):
        j = i // 2
        final = i == depth - 1
        if final:
            out_norm_g = final_norm
        else:
            out_norm_g = ret_norm[j] if i % 2 == 0 else gla_norm[j + 1]
        if i % 2 == 0:
            out = _gla_layer(h, batch, seq, gla_norm[j], gla_w_in[j], gla_w_a1[j], gla_w_a2[j], gla_b_a[j],
                             gla_head_g[j], gla_w_out[j], out_norm_g[None, :], pos, inv_freq, final)
        else:
            out = _ret_layer(h, hn, cos, sin, batch, seq, ret_w_in[j], ret_gn_g[j], ret_gn_b[j], ret_w_out[j],
                             out_norm_g[None, :], pos, inv_freq, final)
        if final:
            h = out
        else:
            h, hn, cos, sin = out
    return h.reshape(batch, seq, d)
```
